```python
import jax, jax.numpy as jnp
from jax import lax
import numpy as np

D_MODEL = 2048
BATCH = 1
SEQ = 16384
DEPTH = 2
DEC_BATCH = 32
DEC_SEQ = 16
PAST_LEN = 2048

CHUNK = 64
N_A_LAYERS = DEPTH // 2
N_B_LAYERS = DEPTH - N_A_LAYERS
D_RNN = D_MODEL
N_LRU_HEADS = 16
LRU_BLOCK = D_RNN // N_LRU_HEADS
CONV_W = 4
LRU_C = 8.0
N_HEADS = 16
HEAD_DIM = D_MODEL // N_HEADS
D_FF = 4 * D_MODEL
Q_BLOCK = 128
EPS = 1e-6

kernel_name = 'yoco_rglru_stickbreaking_stream_step'


def rmsnorm(x, g):
    xf = x.astype(jnp.float32)
    y = xf * lax.rsqrt(jnp.mean(xf * xf, axis=-1, keepdims=True) + EPS)
    return (y * g.astype(jnp.float32)).astype(x.dtype)


def _lin_combine(left, right):
    a1, b1 = left
    a2, b2 = right
    return a1 * a2, a2 * b1 + b2


def conv_rglru(xn, conv_buf, h0, pos0, w_in, conv_w, conv_b, w_r, b_r, w_i, b_i, lam, w_out):
    bn, s, _ = xn.shape
    u = xn @ w_in
    gate = jax.nn.gelu(u[..., :D_RNN])
    rec = u[..., D_RNN:]
    padded = jnp.concatenate([conv_buf.astype(rec.dtype), rec], axis=1)
    new_buf = padded[:, -(CONV_W - 1):]
    c = conv_b + padded[:, 0:s] * conv_w[0]
    for j in range(1, CONV_W):
        c = c + padded[:, j:j + s] * conv_w[j]
    cb = c.reshape(bn, s, N_LRU_HEADS, LRU_BLOCK)
    r = jax.nn.sigmoid((jnp.einsum('bshi,hij->bshj', cb, w_r).reshape(bn, s, D_RNN) + b_r).astype(jnp.float32))
    i = jax.nn.sigmoid((jnp.einsum('bshi,hij->bshj', cb, w_i).reshape(bn, s, D_RNN) + b_i).astype(jnp.float32))
    log_a = -LRU_C * r * jax.nn.softplus(-lam.astype(jnp.float32))
    a = jnp.exp(log_a)
    pos = pos0 + jnp.arange(s)
    mult = jnp.where((pos == 0)[None, :, None], 1.0, jnp.sqrt(-jnp.expm1(2.0 * log_a)))
    b = mult * i * c.astype(jnp.float32)
    b = b.at[:, 0].add(a[:, 0] * h0.astype(jnp.float32))
    _, h = lax.associative_scan(_lin_combine, (a, b), axis=1)
    y = (h.astype(xn.dtype) * gate) @ w_out
    return y, h[:, -1], new_buf


def stick_breaking(q, k, v, q_pos0):
    nq = q.shape[1]
    nk = k.shape[1]
    z = jnp.einsum('bqhd,bkhd->bhqk', q, k).astype(jnp.float32) * (HEAD_DIM ** -0.5)
    valid = jnp.arange(nk)[None, :] < (q_pos0 + jnp.arange(nq))[:, None]
    log_keep = jnp.where(valid, jax.nn.log_sigmoid(-z), 0.0)
    suffix = lax.cumsum(log_keep, axis=3, reverse=True)
    suffix_excl = jnp.concatenate([suffix[..., 1:], jnp.zeros_like(suffix[..., :1])], axis=-1)
    w = jnp.where(valid, jnp.exp(jax.nn.log_sigmoid(z) + suffix_excl), 0.0)
    return jnp.einsum('bhqk,bkhd->bqhd', w.astype(v.dtype), v)


def sq_relu_mlp(x, g, w_up, w_down):
    h = jax.nn.relu(rmsnorm(x, g) @ w_up)
    return (h * h) @ w_down


def trunk(x, conv_bufs, h0s, pos0, past_k, past_v, p):
    bn, s, _ = x.shape
    new_h, new_buf = [], []
    k_new = v_new = k_all = v_all = None
    for layer in range(DEPTH):
        if layer < N_A_LAYERS:
            l = layer
            y, h_last, buf = conv_rglru(rmsnorm(x, p['a_norm'][l]), conv_bufs[l], h0s[l], pos0,
                                        p['a_w_in'][l], p['a_conv_w'][l], p['a_conv_b'][l],
                                        p['a_w_r'][l], p['a_b_r'][l], p['a_w_i'][l], p['a_b_i'][l],
                                        p['a_lambda'][l], p['a_w_out'][l])
            x = x + y
            new_h.append(h_last.astype(x.dtype))
            new_buf.append(buf)
        else:
            if layer == N_A_LAYERS:
                kv = (rmsnorm(x, p['kv_norm']) @ p['w_kv']).reshape(bn, s, 2, N_HEADS, HEAD_DIM)
                k_new = rmsnorm(kv[:, :, 0], p['k_norm'])
                v_new = kv[:, :, 1]
                if past_k is None:
                    k_all, v_all = k_new, v_new
                else:
                    k_all = jnp.concatenate([past_k.astype(k_new.dtype), k_new], axis=1)
                    v_all = jnp.concatenate([past_v.astype(v_new.dtype), v_new], axis=1)
            l = layer - N_A_LAYERS
            q = (rmsnorm(x, p['b_norm'][l]) @ p['b_w_q'][l]).reshape(bn, s, N_HEADS, HEAD_DIM)
            q = rmsnorm(q, p['b_q_norm'][l])
            if past_k is None:
                nb = s // Q_BLOCK
                qb = q.reshape(bn, nb, Q_BLOCK, N_HEADS, HEAD_DIM).swapaxes(0, 1)
                ob = lax.map(lambda args: stick_breaking(args[0], k_all, v_all, args[1]),
                             (qb, jnp.arange(nb) * Q_BLOCK))
                o = ob.swapaxes(0, 1).reshape(bn, s, N_HEADS * HEAD_DIM)
            else:
                o = stick_breaking(q, k_all, v_all, pos0).reshape(bn, s, N_HEADS * HEAD_DIM)
            x = x + o @ p['b_w_o'][l]
        x = x + sq_relu_mlp(x, p['mlp_norm'][layer], p['mlp_w_up'][layer], p['mlp_w_down'][layer])
    return x, jnp.stack(new_h), jnp.stack(new_buf), k_new, v_new


def setup_inputs(seed: int = 0) -> dict:
    key = jax.random.key(seed)
    ks = jax.random.split(key, 32)
    f32 = jnp.float32

    def nrm(k, shape, scale):
        return jax.random.normal(k, shape, f32) * scale

    def gain(k, shape):
        return 1.0 + 0.01 * jax.random.normal(k, shape, f32)

    a_c = jax.random.uniform(ks[10], (N_A_LAYERS, D_RNN), f32, 0.81, 0.998)
    base = a_c ** (1.0 / LRU_C)
    a_lambda = jnp.log(base) - jnp.log1p(-base)
    return {
        'x_prompt': nrm(ks[0], (BATCH, SEQ, D_MODEL), 1.0),
        'x_sample': nrm(ks[1], (DEC_BATCH, DEC_SEQ, D_MODEL), 1.0),
        'state_lru_h': nrm(ks[2], (N_A_LAYERS, DEC_BATCH, D_RNN), 0.5),
        'state_conv': nrm(ks[3], (N_A_LAYERS, DEC_BATCH, CONV_W - 1, D_RNN), 1.0),
        'cache_k': nrm(ks[4], (DEC_BATCH, PAST_LEN, N_HEADS, HEAD_DIM), 1.0),
        'cache_v': nrm(ks[5], (DEC_BATCH, PAST_LEN, N_HEADS, HEAD_DIM), 1.0),
        'a_norm': gain(ks[6], (N_A_LAYERS, D_MODEL)),
        'a_w_in': nrm(ks[7], (N_A_LAYERS, D_MODEL, 2 * D_RNN), D_MODEL ** -0.5),
        'a_conv_w': nrm(ks[8], (N_A_LAYERS, CONV_W, D_RNN), CONV_W ** -0.5),
        'a_conv_b': nrm(ks[9], (N_A_LAYERS, D_RNN), 0.01),
        'a_w_r': nrm(ks[11], (N_A_LAYERS, N_LRU_HEADS, LRU_BLOCK, LRU_BLOCK), LRU_BLOCK ** -0.5),
        'a_b_r': nrm(ks[12], (N_A_LAYERS, D_RNN), 0.01),
        'a_w_i': nrm(ks[13], (N_A_LAYERS, N_LRU_HEADS, LRU_BLOCK, LRU_BLOCK), LRU_BLOCK ** -0.5),
        'a_b_i': nrm(ks[14], (N_A_LAYERS, D_RNN), 0.01),
        'a_lambda': a_lambda,
        'a_w_out': nrm(ks[15], (N_A_LAYERS, D_RNN, D_MODEL), D_RNN ** -0.5),
        'kv_norm': gain(ks[16], (D_MODEL,)),
        'w_kv': nrm(ks[17], (D_MODEL, 2 * N_HEADS * HEAD_DIM), D_MODEL ** -0.5),
        'k_norm': gain(ks[18], (HEAD_DIM,)),
        'b_norm': gain(ks[19], (N_B_LAYERS, D_MODEL)),
        'b_w_q': nrm(ks[20], (N_B_LAYERS, D_MODEL, N_HEADS * HEAD_DIM), D_MODEL ** -0.5),
        'b_q_norm': gain(ks[21], (N_B_LAYERS, HEAD_DIM)),
        'b_w_o': nrm(ks[22], (N_B_LAYERS, N_HEADS * HEAD_DIM, D_MODEL), (N_HEADS * HEAD_DIM) ** -0.5),
        'mlp_norm': gain(ks[23], (DEPTH, D_MODEL)),
        'mlp_w_up': nrm(ks[24], (DEPTH, D_MODEL, D_FF), D_MODEL ** -0.5),
        'mlp_w_down': nrm(ks[25], (DEPTH, D_FF, D_MODEL), D_FF ** -0.5),
    }


def reference(x_prompt, x_sample, state_lru_h, state_conv, cache_k, cache_v,
              a_norm, a_w_in, a_conv_w, a_conv_b, a_w_r, a_b_r, a_w_i, a_b_i, a_lambda, a_w_out,
              kv_norm, w_kv, k_norm, b_norm, b_w_q, b_q_norm, b_w_o,
              mlp_norm, mlp_w_up, mlp_w_down):
    p = {'a_norm': a_norm, 'a_w_in': a_w_in, 'a_conv_w': a_conv_w, 'a_conv_b': a_conv_b,
         'a_w_r': a_w_r, 'a_b_r': a_b_r, 'a_w_i': a_w_i, 'a_b_i': a_b_i, 'a_lambda': a_lambda,
         'a_w_out': a_w_out, 'kv_norm': kv_norm, 'w_kv': w_kv, 'k_norm': k_norm,
         'b_norm': b_norm, 'b_w_q': b_w_q, 'b_q_norm': b_q_norm, 'b_w_o': b_w_o,
         'mlp_norm': mlp_norm, 'mlp_w_up': mlp_w_up, 'mlp_w_down': mlp_w_down}
    bp = x_prompt.shape[0]
    zero_conv = jnp.zeros((N_A_LAYERS, bp, CONV_W - 1, D_RNN), x_prompt.dtype)
    zero_h = jnp.zeros((N_A_LAYERS, bp, D_RNN), x_prompt.dtype)
    y_prompt, p_lru_h, p_conv, p_k, p_v = trunk(x_prompt, zero_conv, zero_h, 0, None, None, p)
    y_sample, s_lru_h, s_conv, s_k, s_v = trunk(x_sample, state_conv, state_lru_h, PAST_LEN,
                                                 cache_k, cache_v, p)
    return (y_prompt, y_sample, p_lru_h, p_conv, p_k, p_v, s_lru_h, s_conv, s_k, s_v)
```

```python
import functools

import jax
import jax.numpy as jnp
from jax import lax
from jax.experimental import pallas as pl
from jax.experimental.pallas import tpu as pltpu

F32 = jnp.float32
BF16 = jnp.bfloat16

D_MODEL = 2048
D_RNN = D_MODEL
N_HEADS = 16
HEAD_DIM = D_MODEL // N_HEADS
N_LRU_HEADS = 16
LRU_BLOCK = D_RNN // N_LRU_HEADS
CONV_W = 4
LRU_C = 8.0
D_FF = 4 * D_MODEL
PAST_LEN = 2048
EPS = 1e-6

V7X_VMEM_LIMIT_BYTES = 56 * 1024 * 1024
SUBLANES = 8


def _params(*sem):
    return pltpu.CompilerParams(dimension_semantics=sem, vmem_limit_bytes=V7X_VMEM_LIMIT_BYTES)


def _rmsnorm(x, g):
    y = x * lax.rsqrt(jnp.mean(x * x, axis=-1, keepdims=True) + EPS)
    return y * g


def _softplus(x):
    return jnp.maximum(x, 0.0) + jnp.log1p(jnp.exp(-jnp.abs(x)))


def _head_rmsnorm(t, g):
    outs = []
    for k in range(t.shape[1] // HEAD_DIM):
        outs.append(_rmsnorm(t[:, k * HEAD_DIM:(k + 1) * HEAD_DIM], g))
    return jnp.concatenate(outs, axis=1) if len(outs) > 1 else outs[0]


def _in_proj_kernel(x_ref, g_ref, wg_ref, wr_ref, gate_ref, rec_ref, xn_ref):
    @pl.when(pl.program_id(1) == 0)
    def _():
        xn_ref[...] = _rmsnorm(x_ref[...], g_ref[...]).astype(BF16)

    xn = xn_ref[...]
    gate_ref[...] = jax.nn.gelu(jnp.dot(xn, wg_ref[...], preferred_element_type=F32))
    rec_ref[...] = jnp.dot(xn, wr_ref[...], preferred_element_type=F32)


def _in_proj(x, g, w, tm, tn):
    t, d = x.shape
    nh = D_RNN // tn
    return pl.pallas_call(
        _in_proj_kernel,
        grid=(t // tm, nh),
        in_specs=[
            pl.BlockSpec((tm, d), lambda i, j: (i, 0)),
            pl.BlockSpec((1, d), lambda i, j: (0, 0)),
            pl.BlockSpec((d, tn), lambda i, j: (0, j)),
            pl.BlockSpec((d, tn), lambda i, j: (0, j + nh)),
        ],
        out_specs=[
            pl.BlockSpec((tm, tn), lambda i, j: (i, j)),
            pl.BlockSpec((tm, tn), lambda i, j: (i, j)),
        ],
        out_shape=[jax.ShapeDtypeStruct((t, D_RNN), F32)] * 2,
        scratch_shapes=[pltpu.VMEM((tm, d), BF16)],
        compiler_params=_params("parallel", "arbitrary"),
        name="in_proj",
    )(x, g, w, w)


def _rglru_kernel(rec_ref, gate_ref, cbuf_ref, h0_ref, cw_ref, cb_ref, br_ref, bi_ref, lam_ref,
                  wr_ref, wi_ref, hg_ref, hlast_ref, nbuf_ref, ext_ref, hc_ref, *, ts, tc, pos0):
    t = pl.program_id(2)
    pad = SUBLANES

    @pl.when(t == 0)
    def _():
        ext_ref[pad - (CONV_W - 1):pad, :] = cbuf_ref[...]
        hc_ref[...] = h0_ref[...]

    ext_ref[pad:pad + ts, :] = rec_ref[...]
    c = cb_ref[...] + ext_ref[pad - 3:pad - 3 + ts, :] * cw_ref[0:1, :]
    for j in range(1, CONV_W):
        c = c + ext_ref[pad - 3 + j:pad - 3 + j + ts, :] * cw_ref[j:j + 1, :]

    c16 = c.astype(BF16)
    rs, is_ = [], []
    for k in range(tc // LRU_BLOCK):
        blk = c16[:, k * LRU_BLOCK:(k + 1) * LRU_BLOCK]
        rs.append(jnp.dot(blk, wr_ref[k], preferred_element_type=F32))
        is_.append(jnp.dot(blk, wi_ref[k], preferred_element_type=F32))
    r = jax.nn.sigmoid(jnp.concatenate(rs, axis=1) + br_ref[...])
    ig = jax.nn.sigmoid(jnp.concatenate(is_, axis=1) + bi_ref[...])

    log_a = (-LRU_C * r) * _softplus(-lam_ref[...])
    a = jnp.exp(log_a)
    mult = jnp.sqrt(jnp.tanh(-log_a) * (a * a + 1.0))
    row = lax.broadcasted_iota(jnp.int32, (ts, tc), 0)
    if pos0 == 0:
        mult = jnp.where((row == 0) & (t == 0), 1.0, mult)
    b = mult * ig * c

    d = 1
    while d < ts:
        keep = row >= d
        a_sh = jnp.where(keep, pltpu.roll(a, d, 0), 1.0)
        b_sh = jnp.where(keep, pltpu.roll(b, d, 0), 0.0)
        b = a * b_sh + b
        a = a * a_sh
        d *= 2
    h = b + a * hc_ref[...]

    hg_ref[...] = (h * gate_ref[...]).astype(hg_ref.dtype)
    hc_ref[...] = h[ts - 1:ts, :]
    hlast_ref[...] = h[ts - 1:ts, :]
    nbuf_ref[...] = ext_ref[pad + ts - (CONV_W - 1):pad + ts, :]
    ext_ref[0:pad, :] = ext_ref[ts:ts + pad, :]


def _rglru(rec, gate, cbuf, h0, cw, cb, br, bi, lam, wr, wi, pos0, ts, tc):
    bn, s, c = rec.shape
    nblk = tc // LRU_BLOCK
    seq = pl.BlockSpec((None, ts, tc), lambda b, ci, ti: (b, ti, ci))
    vec = pl.BlockSpec((1, tc), lambda b, ci, ti: (0, ci))
    return pl.pallas_call(
        functools.partial(_rglru_kernel, ts=ts, tc=tc, pos0=pos0),
        grid=(bn, c // tc, s // ts),
        in_specs=[
            seq, seq,
            pl.BlockSpec((None, CONV_W - 1, tc), lambda b, ci, ti: (b, 0, ci)),
            pl.BlockSpec((None, 1, tc), lambda b, ci, ti: (b, 0, ci)),
            pl.BlockSpec((CONV_W, tc), lambda b, ci, ti: (0, ci)),
            vec, vec, vec, vec,
            pl.BlockSpec((nblk, LRU_BLOCK, LRU_BLOCK), lambda b, ci, ti: (ci, 0, 0)),
            pl.BlockSpec((nblk, LRU_BLOCK, LRU_BLOCK), lambda b, ci, ti: (ci, 0, 0)),
        ],
        out_specs=[
            seq,
            pl.BlockSpec((None, 1, tc), lambda b, ci, ti: (b, 0, ci)),
            pl.BlockSpec((None, CONV_W - 1, tc), lambda b, ci, ti: (b, 0, ci)),
        ],
        out_shape=[
            jax.ShapeDtypeStruct((bn, s, c), BF16),
            jax.ShapeDtypeStruct((bn, 1, c), F32),
            jax.ShapeDtypeStruct((bn, CONV_W - 1, c), F32),
        ],
        scratch_shapes=[pltpu.VMEM((ts + 2 * SUBLANES, tc), F32), pltpu.VMEM((1, tc), F32)],
        compiler_params=_params("parallel", "parallel", "arbitrary"),
        name="rglru",
    )(rec, gate, cbuf, h0, cw, cb, br, bi, lam, wr, wi)


def _matmul_res_kernel(a_ref, w_ref, res_ref, o_ref):
    o_ref[...] = res_ref[...] + jnp.dot(a_ref[...], w_ref[...], preferred_element_type=F32)


def _matmul_res(a, w, res, tm, tn):
    t, k = a.shape
    n = w.shape[1]
    return pl.pallas_call(
        _matmul_res_kernel,
        grid=(t // tm, n // tn),
        in_specs=[
            pl.BlockSpec((tm, k), lambda i, j: (i, 0)),
            pl.BlockSpec((k, tn), lambda i, j: (0, j)),
            pl.BlockSpec((tm, tn), lambda i, j: (i, j)),
        ],
        out_specs=pl.BlockSpec((tm, tn), lambda i, j: (i, j)),
        out_shape=jax.ShapeDtypeStruct((t, n), F32),
        compiler_params=_params("parallel", "arbitrary"),
        name="matmul_res",
    )(a, w, res)


def _mlp_kernel(x_ref, g_ref, wu_ref, wd_ref, o_ref, xn_ref):
    @pl.when(pl.program_id(1) == 0)
    def _():
        x = x_ref[...]
        xn_ref[...] = _rmsnorm(x, g_ref[...]).astype(BF16)
        o_ref[...] = x

    h = jnp.maximum(jnp.dot(xn_ref[...], wu_ref[...], preferred_element_type=F32), 0.0)
    o_ref[...] += jnp.dot((h * h).astype(BF16), wd_ref[...], preferred_element_type=F32)


def _mlp(x, g, wu, wd, tm, tf):
    t, d = x.shape
    f = wu.shape[1]
    return pl.pallas_call(
        _mlp_kernel,
        grid=(t // tm, f // tf),
        in_specs=[
            pl.BlockSpec((tm, d), lambda i, j: (i, 0)),
            pl.BlockSpec((1, d), lambda i, j: (0, 0)),
            pl.BlockSpec((d, tf), lambda i, j: (0, j)),
            pl.BlockSpec((tf, d), lambda i, j: (j, 0)),
        ],
        out_specs=pl.BlockSpec((tm, d), lambda i, j: (i, 0)),
        out_shape=jax.ShapeDtypeStruct((t, d), F32),
        scratch_shapes=[pltpu.VMEM((tm, d), BF16)],
        compiler_params=_params("parallel", "arbitrary"),
        name="mlp",
    )(x, g, wu, wd)


def _kvq_kernel(x_ref, gkv_ref, gq_ref, wk_ref, wv_ref, wq_ref, kn_ref, qn_ref,
                k_ref, v_ref, kh_ref, vh_ref, qh_ref, xkv_ref, xq_ref):
    @pl.when(pl.program_id(1) == 0)
    def _():
        x = x_ref[...]
        xhat = x * lax.rsqrt(jnp.mean(x * x, axis=-1, keepdims=True) + EPS)
        xkv_ref[...] = (xhat * gkv_ref[...]).astype(BF16)
        xq_ref[...] = (xhat * gq_ref[...]).astype(BF16)

    xkv = xkv_ref[...]
    k = _head_rmsnorm(jnp.dot(xkv, wk_ref[...], preferred_element_type=F32), kn_ref[...])
    v = jnp.dot(xkv, wv_ref[...], preferred_element_type=F32)
    q = _head_rmsnorm(jnp.dot(xq_ref[...], wq_ref[...], preferred_element_type=F32), qn_ref[...])
    k_ref[...] = k
    v_ref[...] = v
    for hh in range(kh_ref.shape[0]):
        sl = slice(hh * HEAD_DIM, (hh + 1) * HEAD_DIM)
        kh_ref[hh] = k[:, sl].astype(BF16)
        vh_ref[hh] = v[:, sl].astype(BF16)
        qh_ref[hh] = q[:, sl].astype(BF16)


def _kvq(x, gkv, gq, wkv, wq, kn, qn, tm, tn):
    t, d = x.shape
    nh = D_MODEL // tn
    hb = tn // HEAD_DIM
    tile = pl.BlockSpec((tm, tn), lambda i, j: (i, j))
    heads = pl.BlockSpec((hb, tm, HEAD_DIM), lambda i, j: (j, i, 0))
    vec = pl.BlockSpec((1, d), lambda i, j: (0, 0))
    hvec = pl.BlockSpec((1, HEAD_DIM), lambda i, j: (0, 0))
    return pl.pallas_call(
        _kvq_kernel,
        grid=(t // tm, nh),
        in_specs=[
            pl.BlockSpec((tm, d), lambda i, j: (i, 0)),
            vec, vec,
            pl.BlockSpec((d, tn), lambda i, j: (0, j)),
            pl.BlockSpec((d, tn), lambda i, j: (0, j + nh)),
            pl.BlockSpec((d, tn), lambda i, j: (0, j)),
            hvec, hvec,
        ],
        out_specs=[tile, tile, heads, heads, heads],
        out_shape=[
            jax.ShapeDtypeStruct((t, D_MODEL), F32),
            jax.ShapeDtypeStruct((t, D_MODEL), F32),
            jax.ShapeDtypeStruct((N_HEADS, t, HEAD_DIM), BF16),
            jax.ShapeDtypeStruct((N_HEADS, t, HEAD_DIM), BF16),
            jax.ShapeDtypeStruct((N_HEADS, t, HEAD_DIM), BF16),
        ],
        scratch_shapes=[pltpu.VMEM((tm, d), BF16), pltpu.VMEM((tm, d), BF16)],
        compiler_params=_params("parallel", "arbitrary"),
        name="kvq",
    )(x, gkv, gq, wkv, wkv, wq, kn, qn)


def _suffix_matrix(n):
    r = lax.broadcasted_iota(jnp.int32, (n, n), 0)
    c = lax.broadcasted_iota(jnp.int32, (n, n), 1)
    return jnp.where(r > c, 1.0, 0.0).astype(BF16)


def _sb_block(q, k, v, u, carry, acc, valid):
    z = lax.dot_general(q, k, (((1,), (1,)), ((), ())), preferred_element_type=F32)
    z = z * (HEAD_DIM ** -0.5)
    sp = _softplus(z)
    log_keep = -sp
    log_beta = z - sp
    if valid is not None:
        log_keep = jnp.where(valid, log_keep, 0.0)
    hi = log_keep.astype(BF16)
    lo = (log_keep - hi.astype(F32)).astype(BF16)
    later = (jnp.dot(hi, u, preferred_element_type=F32) + jnp.dot(lo, u, preferred_element_type=F32))
    w = jnp.exp(log_beta + later + carry)
    if valid is not None:
        w = jnp.where(valid, w, 0.0)
    acc = acc + jnp.dot(w.astype(BF16), v, preferred_element_type=F32)
    carry = carry + later[:, 0:1] + log_keep[:, 0:1]
    return carry, acc


def _attn_prompt_kernel(q_ref, k_ref, v_ref, o_ref, *, tq):
    qi = pl.program_id(1)
    q = q_ref[...]
    u = _suffix_matrix(tq)
    r = lax.broadcasted_iota(jnp.int32, (tq, tq), 0)
    c = lax.broadcasted_iota(jnp.int32, (tq, tq), 1)
    start = pl.multiple_of(qi * tq, tq)
    carry, acc = _sb_block(q, k_ref[pl.ds(start, tq), :], v_ref[pl.ds(start, tq), :], u,
                           jnp.zeros((tq, 1), F32), jnp.zeros((tq, HEAD_DIM), F32), c < r)

    def body(it, state):
        off = pl.multiple_of((qi - 1 - it) * tq, tq)
        return _sb_block(q, k_ref[pl.ds(off, tq), :], v_ref[pl.ds(off, tq), :], u, *state, None)

    carry, acc = lax.fori_loop(0, qi, body, (carry, acc))
    o_ref[...] = acc.astype(o_ref.dtype)


def _attn_prompt(qh, kh, vh, tq):
    h, t, hd = qh.shape
    return pl.pallas_call(
        functools.partial(_attn_prompt_kernel, tq=tq),
        grid=(h, t // tq),
        in_specs=[
            pl.BlockSpec((None, tq, hd), lambda hi, qi: (hi, qi, 0)),
            pl.BlockSpec((None, t, hd), lambda hi, qi: (hi, 0, 0)),
            pl.BlockSpec((None, t, hd), lambda hi, qi: (hi, 0, 0)),
        ],
        out_specs=pl.BlockSpec((tq, hd), lambda hi, qi: (qi, hi)),
        out_shape=jax.ShapeDtypeStruct((t, h * hd), BF16),
        compiler_params=_params("parallel", "arbitrary"),
        name="attn_prompt",
    )(qh, kh, vh)


def _attn_sample_kernel(q_ref, kn_ref, vn_ref, ck_ref, cv_ref, o_ref, carry_ref, acc_ref,
                        *, s, tp, tk, pad):
    j = pl.program_id(1)
    n_heads = q_ref.shape[0]

    @pl.when(j == 0)
    def _():
        u_new = _suffix_matrix(pad)
        r = lax.broadcasted_iota(jnp.int32, (s, pad), 0)
        c = lax.broadcasted_iota(jnp.int32, (s, pad), 1)
        zeros = jnp.zeros((pad - s, HEAD_DIM), BF16)

        def first(h, _):
            k_new = jnp.concatenate([kn_ref[h], zeros], axis=0)
            v_new = jnp.concatenate([vn_ref[h], zeros], axis=0)
            carry, acc = _sb_block(q_ref[h], k_new, v_new, u_new, jnp.zeros((s, 1), F32),
                                   jnp.zeros((s, HEAD_DIM), F32), c < r)
            carry_ref[h] = carry
            acc_ref[h] = acc
            return 0

        lax.fori_loop(0, n_heads, first, 0)

    u = _suffix_matrix(tk)

    def head(h, _):
        state = (carry_ref[h], acc_ref[h])
        for blk in range(tp // tk - 1, -1, -1):
            k = ck_ref[blk * tk:(blk + 1) * tk, h, :].astype(BF16)
            v = cv_ref[blk * tk:(blk + 1) * tk, h, :].astype(BF16)
            state = _sb_block(q_ref[h], k, v, u, *state, None)
        carry_ref[h] = state[0]
        acc_ref[h] = state[1]
        return 0

    lax.fori_loop(0, n_heads, head, 0)

    @pl.when(j == pl.num_programs(1) - 1)
    def _():
        for h in range(n_heads):
            o_ref[:, h * HEAD_DIM:(h + 1) * HEAD_DIM] = acc_ref[h].astype(o_ref.dtype)


def _attn_sample(qh, kh, vh, cache_k, cache_v, s, tp, tk):
    h, t, hd = qh.shape
    bn, past = cache_k.shape[:2]
    nkb = past // tp
    new = pl.BlockSpec((h, s, hd), lambda b, j: (0, b, 0))
    old = pl.BlockSpec((None, tp, h, hd), lambda b, j: (b, nkb - 1 - j, 0, 0))
    return pl.pallas_call(
        functools.partial(_attn_sample_kernel, s=s, tp=tp, tk=tk, pad=HEAD_DIM),
        grid=(bn, nkb),
        in_specs=[new, new, new, old, old],
        out_specs=pl.BlockSpec((s, h * hd), lambda b, j: (b, 0)),
        out_shape=jax.ShapeDtypeStruct((t, h * hd), BF16),
        scratch_shapes=[pltpu.VMEM((h, s, 1), F32), pltpu.VMEM((h, s, hd), F32)],
        compiler_params=_params("parallel", "arbitrary"),
        name="attn_sample",
    )(qh, kh, vh, cache_k, cache_v)


def _trunk(x, conv_buf, h0, pos0, cache, p, tm, ts):
    bn, s, d = x.shape
    t = bn * s
    x0 = x.reshape(t, d)

    gate, rec = _in_proj(x0, p["a_norm"], p["a_w_in"], tm, 512)
    hg, h_last, new_buf = _rglru(
        rec.reshape(bn, s, d), gate.reshape(bn, s, d), conv_buf, h0.reshape(bn, 1, d),
        p["a_conv_w"], p["a_conv_b"], p["a_b_r"], p["a_b_i"], p["a_lambda"], p["a_w_r"], p["a_w_i"],
        pos0, ts, 512)
    x1 = _matmul_res(hg.reshape(t, d), p["a_w_out"], x0, tm, 1024)
    x2 = _mlp(x1, p["mlp_norm0"], p["mlp_w_up0"], p["mlp_w_down0"], tm, 512)

    k, v, kh, vh, qh = _kvq(x2, p["kv_norm"], p["b_norm"], p["w_kv"], p["b_w_q"],
                            p["k_norm"], p["b_q_norm"], tm, 512)
    if cache is None:
        o = _attn_prompt(qh, kh, vh, 256)
    else:
        o = _attn_sample(qh, kh, vh, cache[0], cache[1], s, 512, 256)
    x3 = _matmul_res(o, p["b_w_o"], x2, tm, 1024)
    y = _mlp(x3, p["mlp_norm1"], p["mlp_w_up1"], p["mlp_w_down1"], tm, 512)

    return (y.reshape(bn, s, d), h_last.reshape(1, bn, d), new_buf.reshape(1, bn, CONV_W - 1, d),
            k.reshape(bn, s, N_HEADS, HEAD_DIM), v.reshape(bn, s, N_HEADS, HEAD_DIM))


def kernel(x_prompt, x_sample, state_lru_h, state_conv, cache_k, cache_v, a_norm, a_w_in, a_conv_w, a_conv_b, a_w_r, a_b_r, a_w_i, a_b_i, a_lambda, a_w_out, kv_norm, w_kv, k_norm, b_norm, b_w_q, b_q_norm, b_w_o, mlp_norm, mlp_w_up, mlp_w_down):
    row = lambda a: a.reshape(1, -1)
    p = {
        "a_norm": row(a_norm[0]), "a_w_in": a_w_in[0].astype(BF16),
        "a_conv_w": a_conv_w[0], "a_conv_b": row(a_conv_b[0]),
        "a_w_r": a_w_r[0].astype(BF16), "a_b_r": row(a_b_r[0]),
        "a_w_i": a_w_i[0].astype(BF16), "a_b_i": row(a_b_i[0]),
        "a_lambda": row(a_lambda[0]), "a_w_out": a_w_out[0].astype(BF16),
        "kv_norm": row(kv_norm), "w_kv": w_kv.astype(BF16), "k_norm": row(k_norm),
        "b_norm": row(b_norm[0]), "b_w_q": b_w_q[0].astype(BF16), "b_q_norm": row(b_q_norm[0]),
        "b_w_o": b_w_o[0].astype(BF16),
        "mlp_norm0": row(mlp_norm[0]), "mlp_w_up0": mlp_w_up[0].astype(BF16),
        "mlp_w_down0": mlp_w_down[0].astype(BF16),
        "mlp_norm1": row(mlp_norm[1]), "mlp_w_up1": mlp_w_up[1].astype(BF16),
        "mlp_w_down1": mlp_w_down[1].astype(BF16),
    }
    bp, sp, d = x_prompt.shape
    bs, ss, _ = x_sample.shape
    zero_conv = jnp.zeros((bp, CONV_W - 1, D_RNN), x_prompt.dtype)
    zero_h = jnp.zeros((bp, D_RNN), x_prompt.dtype)
    y_p, p_h, p_conv, p_k, p_v = _trunk(x_prompt, zero_conv, zero_h, 0, None, p, 512, 256)
    cache = (cache_k, cache_v)
    y_s, s_h, s_conv, s_k, s_v = _trunk(x_sample, state_conv[0], state_lru_h[0], PAST_LEN, cache,
                                        p, 512, ss)
    return (y_p, y_s, p_h, p_conv, p_k, p_v, s_h, s_conv, s_k, s_v)
```

```python
import functools

import jax
import jax.numpy as jnp
from jax import lax
from jax.experimental import pallas as pl
from jax.experimental.pallas import tpu as pltpu

F32 = jnp.float32
BF16 = jnp.bfloat16

D_MODEL = 2048
D_RNN = D_MODEL
N_HEADS = 16
HEAD_DIM = D_MODEL // N_HEADS
N_LRU_HEADS = 16
LRU_BLOCK = D_RNN // N_LRU_HEADS
CONV_W = 4
LRU_C = 8.0
D_FF = 4 * D_MODEL
PAST_LEN = 2048
EPS = 1e-6

V7X_VMEM_LIMIT_BYTES = 56 * 1024 * 1024
SUBLANES = 8


def _params(*sem):
    return pltpu.CompilerParams(dimension_semantics=sem, vmem_limit_bytes=V7X_VMEM_LIMIT_BYTES)


def _rmsnorm(x, g):
    y = x * lax.rsqrt(jnp.mean(x * x, axis=-1, keepdims=True) + EPS)
    return y * g


def _softplus(x):
    return jnp.maximum(x, 0.0) + jnp.log(1.0 + jnp.exp(-jnp.abs(x)))


def _head_rmsnorm(t, g):
    outs = []
    for k in range(t.shape[1] // HEAD_DIM):
        outs.append(_rmsnorm(t[:, k * HEAD_DIM:(k + 1) * HEAD_DIM], g))
    return jnp.concatenate(outs, axis=1) if len(outs) > 1 else outs[0]


def _in_proj_kernel(x_ref, g_ref, wg_ref, wr_ref, gate_ref, rec_ref, xn_ref):
    @pl.when(pl.program_id(1) == 0)
    def _():
        xn_ref[...] = _rmsnorm(x_ref[...], g_ref[...]).astype(BF16)

    xn = xn_ref[...]
    gate_ref[...] = jax.nn.gelu(jnp.dot(xn, wg_ref[...], preferred_element_type=F32))
    rec_ref[...] = jnp.dot(xn, wr_ref[...], preferred_element_type=F32)


def _in_proj(x, g, w, tm, tn):
    t, d = x.shape
    nh = D_RNN // tn
    return pl.pallas_call(
        _in_proj_kernel,
        grid=(t // tm, nh),
        in_specs=[
            pl.BlockSpec((tm, d), lambda i, j: (i, 0)),
            pl.BlockSpec((1, d), lambda i, j: (0, 0)),
            pl.BlockSpec((d, tn), lambda i, j: (0, j)),
            pl.BlockSpec((d, tn), lambda i, j: (0, j + nh)),
        ],
        out_specs=[
            pl.BlockSpec((tm, tn), lambda i, j: (i, j)),
            pl.BlockSpec((tm, tn), lambda i, j: (i, j)),
        ],
        out_shape=[jax.ShapeDtypeStruct((t, D_RNN), F32)] * 2,
        scratch_shapes=[pltpu.VMEM((tm, d), BF16)],
        compiler_params=_params("parallel", "arbitrary"),
        name="in_proj",
    )(x, g, w, w)


def _rglru_kernel(rec_ref, gate_ref, cbuf_ref, h0_ref, cw_ref, cb_ref, br_ref, bi_ref, lam_ref,
                  wr_ref, wi_ref, hg_ref, hlast_ref, nbuf_ref, ext_ref, hc_ref, *, ts, tc, pos0):
    t = pl.program_id(2)
    pad = SUBLANES

    @pl.when(t == 0)
    def _():
        ext_ref[pad - (CONV_W - 1):pad, :] = cbuf_ref[...]
        hc_ref[...] = h0_ref[...]

    ext_ref[pad:pad + ts, :] = rec_ref[...]
    c = cb_ref[...] + ext_ref[pad - 3:pad - 3 + ts, :] * cw_ref[0:1, :]
    for j in range(1, CONV_W):
        c = c + ext_ref[pad - 3 + j:pad - 3 + j + ts, :] * cw_ref[j:j + 1, :]

    c16 = c.astype(BF16)
    rs, is_ = [], []
    for k in range(tc // LRU_BLOCK):
        blk = c16[:, k * LRU_BLOCK:(k + 1) * LRU_BLOCK]
        rs.append(jnp.dot(blk, wr_ref[k], preferred_element_type=F32))
        is_.append(jnp.dot(blk, wi_ref[k], preferred_element_type=F32))
    r = jax.nn.sigmoid(jnp.concatenate(rs, axis=1) + br_ref[...])
    ig = jax.nn.sigmoid(jnp.concatenate(is_, axis=1) + bi_ref[...])

    log_a = (-LRU_C * r) * _softplus(-lam_ref[...])
    a = jnp.exp(log_a)
    mult = jnp.sqrt(jnp.tanh(-log_a) * (a * a + 1.0))
    row = lax.broadcasted_iota(jnp.int32, (ts, tc), 0)
    if pos0 == 0:
        mult = jnp.where((row == 0) & (t == 0), 1.0, mult)
    b = mult * ig * c

    d = 1
    while d < ts:
        keep = row >= d
        a_sh = jnp.where(keep, pltpu.roll(a, d, 0), 1.0)
        b_sh = jnp.where(keep, pltpu.roll(b, d, 0), 0.0)
        b = a * b_sh + b
        a = a * a_sh
        d *= 2
    h = b + a * hc_ref[...]

    hg_ref[...] = (h * gate_ref[...]).astype(hg_ref.dtype)
    hc_ref[...] = h[ts - 1:ts, :]
    hlast_ref[...] = h[ts - 1:ts, :]
    nbuf_ref[...] = ext_ref[pad + ts - (CONV_W - 1):pad + ts, :]
    ext_ref[0:pad, :] = ext_ref[ts:ts + pad, :]


def _rglru(rec, gate, cbuf, h0, cw, cb, br, bi, lam, wr, wi, pos0, ts, tc):
    bn, s, c = rec.shape
    nblk = tc // LRU_BLOCK
    seq = pl.BlockSpec((None, ts, tc), lambda b, ci, ti: (b, ti, ci))
    vec = pl.BlockSpec((1, tc), lambda b, ci, ti: (0, ci))
    return pl.pallas_call(
        functools.partial(_rglru_kernel, ts=ts, tc=tc, pos0=pos0),
        grid=(bn, c // tc, s // ts),
        in_specs=[
            seq, seq,
            pl.BlockSpec((None, CONV_W - 1, tc), lambda b, ci, ti: (b, 0, ci)),
            pl.BlockSpec((None, 1, tc), lambda b, ci, ti: (b, 0, ci)),
            pl.BlockSpec((CONV_W, tc), lambda b, ci, ti: (0, ci)),
            vec, vec, vec, vec,
            pl.BlockSpec((nblk, LRU_BLOCK, LRU_BLOCK), lambda b, ci, ti: (ci, 0, 0)),
            pl.BlockSpec((nblk, LRU_BLOCK, LRU_BLOCK), lambda b, ci, ti: (ci, 0, 0)),
        ],
        out_specs=[
            seq,
            pl.BlockSpec((None, 1, tc), lambda b, ci, ti: (b, 0, ci)),
            pl.BlockSpec((None, CONV_W - 1, tc), lambda b, ci, ti: (b, 0, ci)),
        ],
        out_shape=[
            jax.ShapeDtypeStruct((bn, s, c), BF16),
            jax.ShapeDtypeStruct((bn, 1, c), F32),
            jax.ShapeDtypeStruct((bn, CONV_W - 1, c), F32),
        ],
        scratch_shapes=[pltpu.VMEM((ts + 2 * SUBLANES, tc), F32), pltpu.VMEM((1, tc), F32)],
        compiler_params=_params("parallel", "parallel", "arbitrary"),
        name="rglru",
    )(rec, gate, cbuf, h0, cw, cb, br, bi, lam, wr, wi)


def _matmul_res_kernel(a_ref, w_ref, res_ref, o_ref):
    o_ref[...] = res_ref[...] + jnp.dot(a_ref[...], w_ref[...], preferred_element_type=F32)


def _matmul_res(a, w, res, tm, tn):
    t, k = a.shape
    n = w.shape[1]
    return pl.pallas_call(
        _matmul_res_kernel,
        grid=(t // tm, n // tn),
        in_specs=[
            pl.BlockSpec((tm, k), lambda i, j: (i, 0)),
            pl.BlockSpec((k, tn), lambda i, j: (0, j)),
            pl.BlockSpec((tm, tn), lambda i, j: (i, j)),
        ],
        out_specs=pl.BlockSpec((tm, tn), lambda i, j: (i, j)),
        out_shape=jax.ShapeDtypeStruct((t, n), F32),
        compiler_params=_params("parallel", "arbitrary"),
        name="matmul_res",
    )(a, w, res)


def _mlp_kernel(x_ref, g_ref, wu_ref, wd_ref, o_ref, xn_ref):
    @pl.when(pl.program_id(1) == 0)
    def _():
        x = x_ref[...]
        xn_ref[...] = _rmsnorm(x, g_ref[...]).astype(BF16)
        o_ref[...] = x

    h = jnp.maximum(jnp.dot(xn_ref[...], wu_ref[...], preferred_element_type=F32), 0.0)
    o_ref[...] += jnp.dot((h * h).astype(BF16), wd_ref[...], preferred_element_type=F32)


def _mlp(x, g, wu, wd, tm, tf):
    t, d = x.shape
    f = wu.shape[1]
    return pl.pallas_call(
        _mlp_kernel,
        grid=(t // tm, f // tf),
        in_specs=[
            pl.BlockSpec((tm, d), lambda i, j: (i, 0)),
            pl.BlockSpec((1, d), lambda i, j: (0, 0)),
            pl.BlockSpec((d, tf), lambda i, j: (0, j)),
            pl.BlockSpec((tf, d), lambda i, j: (j, 0)),
        ],
        out_specs=pl.BlockSpec((tm, d), lambda i, j: (i, 0)),
        out_shape=jax.ShapeDtypeStruct((t, d), F32),
        scratch_shapes=[pltpu.VMEM((tm, d), BF16)],
        compiler_params=_params("parallel", "arbitrary"),
        name="mlp",
    )(x, g, wu, wd)


def _kvq_kernel(x_ref, gkv_ref, gq_ref, wk_ref, wv_ref, wq_ref, kn_ref, qn_ref,
                k_ref, v_ref, kh_ref, vh_ref, qh_ref, xkv_ref, xq_ref):
    @pl.when(pl.program_id(1) == 0)
    def _():
        x = x_ref[...]
        xhat = x * lax.rsqrt(jnp.mean(x * x, axis=-1, keepdims=True) + EPS)
        xkv_ref[...] = (xhat * gkv_ref[...]).astype(BF16)
        xq_ref[...] = (xhat * gq_ref[...]).astype(BF16)

    xkv = xkv_ref[...]
    k = _head_rmsnorm(jnp.dot(xkv, wk_ref[...], preferred_element_type=F32), kn_ref[...])
    v = jnp.dot(xkv, wv_ref[...], preferred_element_type=F32)
    q = _head_rmsnorm(jnp.dot(xq_ref[...], wq_ref[...], preferred_element_type=F32), qn_ref[...])
    k_ref[...] = k
    v_ref[...] = v
    for hh in range(kh_ref.shape[0]):
        sl = slice(hh * HEAD_DIM, (hh + 1) * HEAD_DIM)
        kh_ref[hh] = k[:, sl].astype(BF16)
        vh_ref[hh] = v[:, sl].astype(BF16)
        qh_ref[hh] = q[:, sl].astype(BF16)


def _kvq(x, gkv, gq, wkv, wq, kn, qn, tm, tn):
    t, d = x.shape
    nh = D_MODEL // tn
    hb = tn // HEAD_DIM
    tile = pl.BlockSpec((tm, tn), lambda i, j: (i, j))
    heads = pl.BlockSpec((hb, tm, HEAD_DIM), lambda i, j: (j, i, 0))
    vec = pl.BlockSpec((1, d), lambda i, j: (0, 0))
    hvec = pl.BlockSpec((1, HEAD_DIM), lambda i, j: (0, 0))
    return pl.pallas_call(
        _kvq_kernel,
        grid=(t // tm, nh),
        in_specs=[
            pl.BlockSpec((tm, d), lambda i, j: (i, 0)),
            vec, vec,
            pl.BlockSpec((d, tn), lambda i, j: (0, j)),
            pl.BlockSpec((d, tn), lambda i, j: (0, j + nh)),
            pl.BlockSpec((d, tn), lambda i, j: (0, j)),
            hvec, hvec,
        ],
        out_specs=[tile, tile, heads, heads, heads],
        out_shape=[
            jax.ShapeDtypeStruct((t, D_MODEL), F32),
            jax.ShapeDtypeStruct((t, D_MODEL), F32),
            jax.ShapeDtypeStruct((N_HEADS, t, HEAD_DIM), BF16),
            jax.ShapeDtypeStruct((N_HEADS, t, HEAD_DIM), BF16),
            jax.ShapeDtypeStruct((N_HEADS, t, HEAD_DIM), BF16),
        ],
        scratch_shapes=[pltpu.VMEM((tm, d), BF16), pltpu.VMEM((tm, d), BF16)],
        compiler_params=_params("parallel", "arbitrary"),
        name="kvq",
    )(x, gkv, gq, wkv, wkv, wq, kn, qn)


def _suffix_matrix(n):
    r = lax.broadcasted_iota(jnp.int32, (n, n), 0)
    c = lax.broadcasted_iota(jnp.int32, (n, n), 1)
    return jnp.where(r > c, 1.0, 0.0).astype(BF16)


NEG_BIG = -1e30
LOG2E = 1.4426950408889634


def _sb_logs(zr, valid):
    scale = HEAD_DIM ** -0.5
    nz = zr * (-scale)
    tail = jnp.log(1.0 + jnp.exp2(jnp.abs(zr) * (-scale * LOG2E)))
    log_keep = jnp.minimum(nz, 0.0) - tail
    log_beta = log_keep - nz
    if valid is not None:
        log_keep = jnp.where(valid, log_keep, 0.0)
        log_beta = jnp.where(valid, log_beta, NEG_BIG)
    hi = log_keep.astype(BF16)
    lo = (log_keep - hi.astype(F32)).astype(BF16)
    return hi, lo, log_beta, log_keep[:, 0:1]


def _sb_scores(q, k, valid):
    zr = lax.dot_general(q, k, (((1,), (1,)), ((), ())), preferred_element_type=F32)
    hi, lo, log_beta, _ = _sb_logs(zr, valid)
    return jnp.concatenate([hi, lo], axis=1), log_beta


def _sb_suffix(hilo, log_beta, uu, carry):
    n = log_beta.shape[1]
    later = jnp.dot(hilo, uu, preferred_element_type=F32)
    first = hilo[:, 0:1].astype(F32) + hilo[:, n:n + 1].astype(F32)
    return log_beta + later + carry, carry + later[:, 0:1] + first


def _sb_values(log_w, v, acc):
    return acc + jnp.dot(jnp.exp(log_w).astype(BF16), v, preferred_element_type=F32)


def _sb_block(q, k, v, uu, carry, acc, valid):
    hilo, log_beta = _sb_scores(q, k, valid)
    log_w, carry = _sb_suffix(hilo, log_beta, uu, carry)
    return carry, _sb_values(log_w, v, acc)


def _attn_prompt_kernel(q_ref, k_ref, v_ref, o_ref, *scratch, tq, tk, rc):
    z_ref, hilo_ref, logb_ref, first_ref, later_ref, logw_ref, w_ref, acc_ref, carry_ref = scratch
    qi = pl.program_id(1)
    per = tq // tk
    nb = (qi + 1) * per
    u = _suffix_matrix(tk)
    uu = jnp.concatenate([u, u], axis=0)
    chunks = [slice(r0, r0 + rc) for r0 in range(0, tq, rc)]

    def keys(ref, n):
        return ref[pl.ds(pl.multiple_of((nb - 1 - n) * tk, tk), tk), :]

    def raw_scores(n, slot):
        z_ref[slot] = lax.dot_general(q_ref[...], keys(k_ref, n), (((1,), (1,)), ((), ())),
                                      preferred_element_type=F32)

    def log_terms(slot, masked_from):
        for rows in chunks:
            valid = None
            if masked_from is not None:
                row = lax.broadcasted_iota(jnp.int32, (rc, tk), 0) + rows.start
                col = lax.broadcasted_iota(jnp.int32, (rc, tk), 1) + masked_from
                valid = col < row
            hi, lo, log_beta, first = _sb_logs(z_ref[slot, rows, :], valid)
            hilo_ref[slot, rows, 0:tk] = hi
            hilo_ref[slot, rows, tk:2 * tk] = lo
            logb_ref[slot, rows, :] = log_beta
            first_ref[slot, rows, :] = first

    def suffix_sums(slot):
        later_ref[slot] = jnp.dot(hilo_ref[slot], uu, preferred_element_type=F32)
        for rows in chunks:
            later = later_ref[slot, rows, :]
            carry = carry_ref[rows, :]
            logw_ref[slot, rows, :] = logb_ref[slot, rows, :] + later + carry
            carry_ref[rows, :] = carry + later[:, 0:1] + first_ref[slot, rows, :]

    def values(n, slot):
        for rows in chunks:
            w_ref[slot, rows, :] = jnp.exp(logw_ref[slot, rows, :]).astype(BF16)
        acc_ref[...] += jnp.dot(w_ref[slot], keys(v_ref, n), preferred_element_type=F32)

    def step(t, parity, first_stage=0, last_stage=3, masked_from=None):
        if first_stage <= 0 <= last_stage:
            raw_scores(t, parity)
        if first_stage <= 1 <= last_stage:
            log_terms(1 - parity, masked_from)
        if first_stage <= 2 <= last_stage:
            suffix_sums(parity)
        if first_stage <= 3 <= last_stage:
            values(t - 3, 1 - parity)

    carry_ref[...] = jnp.zeros_like(carry_ref)
    acc_ref[...] = jnp.zeros_like(acc_ref)

    @pl.when(qi == 0)
    def _():
        for n in range(per):
            raw_scores(n, 0)
            log_terms(0, (per - 1 - n) * tk)
            suffix_sums(0)
            values(n, 0)

    @pl.when(qi > 0)
    def _():
        step(0, 0, last_stage=0)
        step(1, 1, last_stage=1, masked_from=(per - 1) * tk)
        step(2, 0, last_stage=2, masked_from=(per - 2) * tk)
        step(3, 1)

        def body(m, _):
            step(4 + 2 * m, 0)
            step(5 + 2 * m, 1)
            return 0

        lax.fori_loop(0, (nb - 4) // 2, body, 0)
        step(nb, 0, first_stage=1)
        step(nb + 1, 1, first_stage=2)
        step(nb + 2, 0, first_stage=3)

    o_ref[...] = acc_ref[...].astype(o_ref.dtype)


def _attn_prompt(qh, kh, vh, tq, tk, rc):
    h, t, hd = qh.shape
    assert tq == 2 * tk and tq % rc == 0
    resident = pl.BlockSpec((None, t, hd), lambda hi, qi: (hi, 0, 0))
    return pl.pallas_call(
        functools.partial(_attn_prompt_kernel, tq=tq, tk=tk, rc=rc),
        grid=(h, t // tq),
        in_specs=[pl.BlockSpec((None, tq, hd), lambda hi, qi: (hi, qi, 0)), resident, resident],
        out_specs=pl.BlockSpec((tq, hd), lambda hi, qi: (qi, hi)),
        out_shape=jax.ShapeDtypeStruct((t, h * hd), BF16),
        scratch_shapes=[
            pltpu.VMEM((2, tq, tk), F32),
            pltpu.VMEM((2, tq, 2 * tk), BF16),
            pltpu.VMEM((2, tq, tk), F32),
            pltpu.VMEM((2, tq, 1), F32),
            pltpu.VMEM((2, tq, tk), F32),
            pltpu.VMEM((2, tq, tk), F32),
            pltpu.VMEM((2, tq, tk), BF16),
            pltpu.VMEM((tq, hd), F32),
            pltpu.VMEM((tq, 1), F32),
        ],
        compiler_params=_params("parallel", "arbitrary"),
        name="attn_prompt",
    )(qh, kh, vh)


def _attn_sample_kernel(q_ref, kn_ref, vn_ref, ck_ref, cv_ref, o_ref, carry_ref, acc_ref,
                        *, s, tp, tk, pad):
    j = pl.program_id(1)
    n_heads = q_ref.shape[0]

    @pl.when(j == 0)
    def _():
        u_new = _suffix_matrix(pad)
        u_new = jnp.concatenate([u_new, u_new], axis=0)
        r = lax.broadcasted_iota(jnp.int32, (s, pad), 0)
        c = lax.broadcasted_iota(jnp.int32, (s, pad), 1)
        zeros = jnp.zeros((pad - s, HEAD_DIM), BF16)

        def first(h, _):
            k_new = jnp.concatenate([kn_ref[h], zeros], axis=0)
            v_new = jnp.concatenate([vn_ref[h], zeros], axis=0)
            carry, acc = _sb_block(q_ref[h], k_new, v_new, u_new, jnp.zeros((s, 1), F32),
                                   jnp.zeros((s, HEAD_DIM), F32), c < r)
            carry_ref[h] = carry
            acc_ref[h] = acc
            return 0

        lax.fori_loop(0, n_heads, first, 0)

    u = _suffix_matrix(tk)
    u = jnp.concatenate([u, u], axis=0)

    def head(h, _):
        state = (carry_ref[h], acc_ref[h])
        for blk in range(tp // tk - 1, -1, -1):
            k = ck_ref[blk * tk:(blk + 1) * tk, h, :].astype(BF16)
            v = cv_ref[blk * tk:(blk + 1) * tk, h, :].astype(BF16)
            state = _sb_block(q_ref[h], k, v, u, *state, None)
        carry_ref[h] = state[0]
        acc_ref[h] = state[1]
        return 0

    lax.fori_loop(0, n_heads, head, 0)

    @pl.when(j == pl.num_programs(1) - 1)
    def _():
        for h in range(n_heads):
            o_ref[:, h * HEAD_DIM:(h + 1) * HEAD_DIM] = acc_ref[h].astype(o_ref.dtype)


def _attn_sample(qh, kh, vh, cache_k, cache_v, s, tp, tk):
    h, t, hd = qh.shape
    bn, past = cache_k.shape[:2]
    nkb = past // tp
    new = pl.BlockSpec((h, s, hd), lambda b, j: (0, b, 0))
    old = pl.BlockSpec((None, tp, h, hd), lambda b, j: (b, nkb - 1 - j, 0, 0))
    return pl.pallas_call(
        functools.partial(_attn_sample_kernel, s=s, tp=tp, tk=tk, pad=HEAD_DIM),
        grid=(bn, nkb),
        in_specs=[new, new, new, old, old],
        out_specs=pl.BlockSpec((s, h * hd), lambda b, j: (b, 0)),
        out_shape=jax.ShapeDtypeStruct((t, h * hd), BF16),
        scratch_shapes=[pltpu.VMEM((h, s, 1), F32), pltpu.VMEM((h, s, hd), F32)],
        compiler_params=_params("parallel", "arbitrary"),
        name="attn_sample",
    )(qh, kh, vh, cache_k, cache_v)


def _trunk(x, conv_buf, h0, pos0, cache, p, tm, ts):
    bn, s, d = x.shape
    t = bn * s
    x0 = x.reshape(t, d)

    gate, rec = _in_proj(x0, p["a_norm"], p["a_w_in"], tm, 512)
    hg, h_last, new_buf = _rglru(
        rec.reshape(bn, s, d), gate.reshape(bn, s, d), conv_buf, h0.reshape(bn, 1, d),
        p["a_conv_w"], p["a_conv_b"], p["a_b_r"], p["a_b_i"], p["a_lambda"], p["a_w_r"], p["a_w_i"],
        pos0, ts, 512)
    x1 = _matmul_res(hg.reshape(t, d), p["a_w_out"], x0, tm, 1024)
    x2 = _mlp(x1, p["mlp_norm0"], p["mlp_w_up0"], p["mlp_w_down0"], tm, 512)

    k, v, kh, vh, qh = _kvq(x2, p["kv_norm"], p["b_norm"], p["w_kv"], p["b_w_q"],
                            p["k_norm"], p["b_q_norm"], tm, 512)
    if cache is None:
        o = _attn_prompt(qh, kh, vh, 512, 256, 32)
    else:
        o = _attn_sample(qh, kh, vh, cache[0], cache[1], s, 512, 256)
    x3 = _matmul_res(o, p["b_w_o"], x2, tm, 1024)
    y = _mlp(x3, p["mlp_norm1"], p["mlp_w_up1"], p["mlp_w_down1"], tm, 512)

    return (y.reshape(bn, s, d), h_last.reshape(1, bn, d), new_buf.reshape(1, bn, CONV_W - 1, d),
            k.reshape(bn, s, N_HEADS, HEAD_DIM), v.reshape(bn, s, N_HEADS, HEAD_DIM))


def kernel(x_prompt, x_sample, state_lru_h, state_conv, cache_k, cache_v, a_norm, a_w_in, a_conv_w, a_conv_b, a_w_r, a_b_r, a_w_i, a_b_i, a_lambda, a_w_out, kv_norm, w_kv, k_norm, b_norm, b_w_q, b_q_norm, b_w_o, mlp_norm, mlp_w_up, mlp_w_down):
    row = lambda a: a.reshape(1, -1)
    p = {
        "a_norm": row(a_norm[0]), "a_w_in": a_w_in[0].astype(BF16),
        "a_conv_w": a_conv_w[0], "a_conv_b": row(a_conv_b[0]),
        "a_w_r": a_w_r[0].astype(BF16), "a_b_r": row(a_b_r[0]),
        "a_w_i": a_w_i[0].astype(BF16), "a_b_i": row(a_b_i[0]),
        "a_lambda": row(a_lambda[0]), "a_w_out": a_w_out[0].astype(BF16),
        "kv_norm": row(kv_norm), "w_kv": w_kv.astype(BF16), "k_norm": row(k_norm),
        "b_norm": row(b_norm[0]), "b_w_q": b_w_q[0].astype(BF16), "b_q_norm": row(b_q_norm[0]),
        "b_w_o": b_w_o[0].astype(BF16),
        "mlp_norm0": row(mlp_norm[0]), "mlp_w_up0": mlp_w_up[0].astype(BF16),
        "mlp_w_down0": mlp_w_down[0].astype(BF16),
        "mlp_norm1": row(mlp_norm[1]), "mlp_w_up1": mlp_w_up[1].astype(BF16),
        "mlp_w_down1": mlp_w_down[1].astype(BF16),
    }
    bp, sp, d = x_prompt.shape
    bs, ss, _ = x_sample.shape
    zero_conv = jnp.zeros((bp, CONV_W - 1, D_RNN), x_prompt.dtype)
    zero_h = jnp.zeros((bp, D_RNN), x_prompt.dtype)
    y_p, p_h, p_conv, p_k, p_v = _trunk(x_prompt, zero_conv, zero_h, 0, None, p, 512, 256)
    cache = (cache_k, cache_v)
    y_s, s_h, s_conv, s_k, s_v = _trunk(x_sample, state_conv[0], state_lru_h[0], PAST_LEN, cache,
                                        p, 512, ss)
    return (y_p, y_s, p_h, p_conv, p_k, p_v, s_h, s_conv, s_k, s_v)
```

```python
import functools

import jax
import jax.numpy as jnp
from jax import lax
from jax.experimental import pallas as pl
from jax.experimental.pallas import tpu as pltpu

F32 = jnp.float32
BF16 = jnp.bfloat16

D_MODEL = 2048
D_RNN = D_MODEL
N_HEADS = 16
HEAD_DIM = D_MODEL // N_HEADS
N_LRU_HEADS = 16
LRU_BLOCK = D_RNN // N_LRU_HEADS
CONV_W = 4
LRU_C = 8.0
D_FF = 4 * D_MODEL
PAST_LEN = 2048
EPS = 1e-6

V7X_VMEM_LIMIT_BYTES = 56 * 1024 * 1024
SUBLANES = 8


def _params(*sem):
    return pltpu.CompilerParams(dimension_semantics=sem, vmem_limit_bytes=V7X_VMEM_LIMIT_BYTES)


def _rmsnorm(x, g):
    y = x * lax.rsqrt(jnp.mean(x * x, axis=-1, keepdims=True) + EPS)
    return y * g


def _softplus(x):
    return jnp.maximum(x, 0.0) + jnp.log(1.0 + jnp.exp(-jnp.abs(x)))


def _head_rmsnorm(t, g):
    outs = []
    for k in range(t.shape[1] // HEAD_DIM):
        outs.append(_rmsnorm(t[:, k * HEAD_DIM:(k + 1) * HEAD_DIM], g))
    return jnp.concatenate(outs, axis=1) if len(outs) > 1 else outs[0]


def _in_proj_kernel(x_ref, g_ref, wg_ref, wr_ref, gate_ref, rec_ref, xn_ref):
    @pl.when(pl.program_id(1) == 0)
    def _():
        xn_ref[...] = _rmsnorm(x_ref[...], g_ref[...]).astype(BF16)

    xn = xn_ref[...]
    gate_ref[...] = jax.nn.gelu(jnp.dot(xn, wg_ref[...], preferred_element_type=F32))
    rec_ref[...] = jnp.dot(xn, wr_ref[...], preferred_element_type=F32)


def _in_proj(x, g, w, tm, tn):
    t, d = x.shape
    nh = D_RNN // tn
    return pl.pallas_call(
        _in_proj_kernel,
        grid=(t // tm, nh),
        in_specs=[
            pl.BlockSpec((tm, d), lambda i, j: (i, 0)),
            pl.BlockSpec((1, d), lambda i, j: (0, 0)),
            pl.BlockSpec((d, tn), lambda i, j: (0, j)),
            pl.BlockSpec((d, tn), lambda i, j: (0, j + nh)),
        ],
        out_specs=[
            pl.BlockSpec((tm, tn), lambda i, j: (i, j)),
            pl.BlockSpec((tm, tn), lambda i, j: (i, j)),
        ],
        out_shape=[jax.ShapeDtypeStruct((t, D_RNN), F32)] * 2,
        scratch_shapes=[pltpu.VMEM((tm, d), BF16)],
        compiler_params=_params("parallel", "arbitrary"),
        name="in_proj",
    )(x, g, w, w)


def _rglru_kernel(rec_ref, gate_ref, cbuf_ref, h0_ref, cw_ref, cb_ref, br_ref, bi_ref, lam_ref,
                  wr_ref, wi_ref, hg_ref, hlast_ref, nbuf_ref, ext_ref, hc_ref, *, ts, tc, pos0):
    t = pl.program_id(2)
    pad = SUBLANES

    @pl.when(t == 0)
    def _():
        ext_ref[pad - (CONV_W - 1):pad, :] = cbuf_ref[...]
        hc_ref[...] = h0_ref[...]

    ext_ref[pad:pad + ts, :] = rec_ref[...]
    c = cb_ref[...] + ext_ref[pad - 3:pad - 3 + ts, :] * cw_ref[0:1, :]
    for j in range(1, CONV_W):
        c = c + ext_ref[pad - 3 + j:pad - 3 + j + ts, :] * cw_ref[j:j + 1, :]

    c16 = c.astype(BF16)
    rs, is_ = [], []
    for k in range(tc // LRU_BLOCK):
        blk = c16[:, k * LRU_BLOCK:(k + 1) * LRU_BLOCK]
        rs.append(jnp.dot(blk, wr_ref[k], preferred_element_type=F32))
        is_.append(jnp.dot(blk, wi_ref[k], preferred_element_type=F32))
    r = jax.nn.sigmoid(jnp.concatenate(rs, axis=1) + br_ref[...])
    ig = jax.nn.sigmoid(jnp.concatenate(is_, axis=1) + bi_ref[...])

    log_a = (-LRU_C * r) * _softplus(-lam_ref[...])
    a = jnp.exp(log_a)
    mult = jnp.sqrt(jnp.tanh(-log_a) * (a * a + 1.0))
    row = lax.broadcasted_iota(jnp.int32, (ts, tc), 0)
    if pos0 == 0:
        mult = jnp.where((row == 0) & (t == 0), 1.0, mult)
    b = mult * ig * c

    d = 1
    while d < ts:
        keep = row >= d
        a_sh = jnp.where(keep, pltpu.roll(a, d, 0), 1.0)
        b_sh = jnp.where(keep, pltpu.roll(b, d, 0), 0.0)
        b = a * b_sh + b
        a = a * a_sh
        d *= 2
    h = b + a * hc_ref[...]

    hg_ref[...] = (h * gate_ref[...]).astype(hg_ref.dtype)
    hc_ref[...] = h[ts - 1:ts, :]
    hlast_ref[...] = h[ts - 1:ts, :]
    nbuf_ref[...] = ext_ref[pad + ts - (CONV_W - 1):pad + ts, :]
    ext_ref[0:pad, :] = ext_ref[ts:ts + pad, :]


def _rglru(rec, gate, cbuf, h0, cw, cb, br, bi, lam, wr, wi, pos0, ts, tc):
    bn, s, c = rec.shape
    nblk = tc // LRU_BLOCK
    seq = pl.BlockSpec((None, ts, tc), lambda b, ci, ti: (b, ti, ci))
    vec = pl.BlockSpec((1, tc), lambda b, ci, ti: (0, ci))
    return pl.pallas_call(
        functools.partial(_rglru_kernel, ts=ts, tc=tc, pos0=pos0),
        grid=(bn, c // tc, s // ts),
        in_specs=[
            seq, seq,
            pl.BlockSpec((None, CONV_W - 1, tc), lambda b, ci, ti: (b, 0, ci)),
            pl.BlockSpec((None, 1, tc), lambda b, ci, ti: (b, 0, ci)),
            pl.BlockSpec((CONV_W, tc), lambda b, ci, ti: (0, ci)),
            vec, vec, vec, vec,
            pl.BlockSpec((nblk, LRU_BLOCK, LRU_BLOCK), lambda b, ci, ti: (ci, 0, 0)),
            pl.BlockSpec((nblk, LRU_BLOCK, LRU_BLOCK), lambda b, ci, ti: (ci, 0, 0)),
        ],
        out_specs=[
            seq,
            pl.BlockSpec((None, 1, tc), lambda b, ci, ti: (b, 0, ci)),
            pl.BlockSpec((None, CONV_W - 1, tc), lambda b, ci, ti: (b, 0, ci)),
        ],
        out_shape=[
            jax.ShapeDtypeStruct((bn, s, c), BF16),
            jax.ShapeDtypeStruct((bn, 1, c), F32),
            jax.ShapeDtypeStruct((bn, CONV_W - 1, c), F32),
        ],
        scratch_shapes=[pltpu.VMEM((ts + 2 * SUBLANES, tc), F32), pltpu.VMEM((1, tc), F32)],
        compiler_params=_params("parallel", "parallel", "arbitrary"),
        name="rglru",
    )(rec, gate, cbuf, h0, cw, cb, br, bi, lam, wr, wi)


def _matmul_res_kernel(a_ref, w_ref, res_ref, o_ref):
    o_ref[...] = res_ref[...] + jnp.dot(a_ref[...], w_ref[...], preferred_element_type=F32)


def _matmul_res(a, w, res, tm, tn):
    t, k = a.shape
    n = w.shape[1]
    return pl.pallas_call(
        _matmul_res_kernel,
        grid=(t // tm, n // tn),
        in_specs=[
            pl.BlockSpec((tm, k), lambda i, j: (i, 0)),
            pl.BlockSpec((k, tn), lambda i, j: (0, j)),
            pl.BlockSpec((tm, tn), lambda i, j: (i, j)),
        ],
        out_specs=pl.BlockSpec((tm, tn), lambda i, j: (i, j)),
        out_shape=jax.ShapeDtypeStruct((t, n), F32),
        compiler_params=_params("parallel", "arbitrary"),
        name="matmul_res",
    )(a, w, res)


def _mlp_kernel(x_ref, g_ref, wu_ref, wd_ref, o_ref, xn_ref):
    @pl.when(pl.program_id(1) == 0)
    def _():
        x = x_ref[...]
        xn_ref[...] = _rmsnorm(x, g_ref[...]).astype(BF16)
        o_ref[...] = x

    h = jnp.maximum(jnp.dot(xn_ref[...], wu_ref[...], preferred_element_type=F32), 0.0)
    o_ref[...] += jnp.dot((h * h).astype(BF16), wd_ref[...], preferred_element_type=F32)


def _mlp(x, g, wu, wd, tm, tf):
    t, d = x.shape
    f = wu.shape[1]
    return pl.pallas_call(
        _mlp_kernel,
        grid=(t // tm, f // tf),
        in_specs=[
            pl.BlockSpec((tm, d), lambda i, j: (i, 0)),
            pl.BlockSpec((1, d), lambda i, j: (0, 0)),
            pl.BlockSpec((d, tf), lambda i, j: (0, j)),
            pl.BlockSpec((tf, d), lambda i, j: (j, 0)),
        ],
        out_specs=pl.BlockSpec((tm, d), lambda i, j: (i, 0)),
        out_shape=jax.ShapeDtypeStruct((t, d), F32),
        scratch_shapes=[pltpu.VMEM((tm, d), BF16)],
        compiler_params=_params("parallel", "arbitrary"),
        name="mlp",
    )(x, g, wu, wd)


def _kvq_kernel(x_ref, gkv_ref, gq_ref, wk_ref, wv_ref, wq_ref, kn_ref, qn_ref,
                k_ref, v_ref, kh_ref, vh_ref, qh_ref, xkv_ref, xq_ref):
    @pl.when(pl.program_id(1) == 0)
    def _():
        x = x_ref[...]
        xhat = x * lax.rsqrt(jnp.mean(x * x, axis=-1, keepdims=True) + EPS)
        xkv_ref[...] = (xhat * gkv_ref[...]).astype(BF16)
        xq_ref[...] = (xhat * gq_ref[...]).astype(BF16)

    xkv = xkv_ref[...]
    k = _head_rmsnorm(jnp.dot(xkv, wk_ref[...], preferred_element_type=F32), kn_ref[...])
    v = jnp.dot(xkv, wv_ref[...], preferred_element_type=F32)
    q = _head_rmsnorm(jnp.dot(xq_ref[...], wq_ref[...], preferred_element_type=F32), qn_ref[...])
    k_ref[...] = k
    v_ref[...] = v
    for hh in range(kh_ref.shape[0]):
        sl = slice(hh * HEAD_DIM, (hh + 1) * HEAD_DIM)
        kh_ref[hh] = k[:, sl].astype(BF16)
        vh_ref[hh] = v[:, sl].astype(BF16)
        qh_ref[hh] = q[:, sl].astype(BF16)


def _kvq(x, gkv, gq, wkv, wq, kn, qn, tm, tn):
    t, d = x.shape
    nh = D_MODEL // tn
    hb = tn // HEAD_DIM
    tile = pl.BlockSpec((tm, tn), lambda i, j: (i, j))
    heads = pl.BlockSpec((hb, tm, HEAD_DIM), lambda i, j: (j, i, 0))
    vec = pl.BlockSpec((1, d), lambda i, j: (0, 0))
    hvec = pl.BlockSpec((1, HEAD_DIM), lambda i, j: (0, 0))
    return pl.pallas_call(
        _kvq_kernel,
        grid=(t // tm, nh),
        in_specs=[
            pl.BlockSpec((tm, d), lambda i, j: (i, 0)),
            vec, vec,
            pl.BlockSpec((d, tn), lambda i, j: (0, j)),
            pl.BlockSpec((d, tn), lambda i, j: (0, j + nh)),
            pl.BlockSpec((d, tn), lambda i, j: (0, j)),
            hvec, hvec,
        ],
        out_specs=[tile, tile, heads, heads, heads],
        out_shape=[
            jax.ShapeDtypeStruct((t, D_MODEL), F32),
            jax.ShapeDtypeStruct((t, D_MODEL), F32),
            jax.ShapeDtypeStruct((N_HEADS, t, HEAD_DIM), BF16),
            jax.ShapeDtypeStruct((N_HEADS, t, HEAD_DIM), BF16),
            jax.ShapeDtypeStruct((N_HEADS, t, HEAD_DIM), BF16),
        ],
        scratch_shapes=[pltpu.VMEM((tm, d), BF16), pltpu.VMEM((tm, d), BF16)],
        compiler_params=_params("parallel", "arbitrary"),
        name="kvq",
    )(x, gkv, gq, wkv, wkv, wq, kn, qn)


def _suffix_matrix(n):
    r = lax.broadcasted_iota(jnp.int32, (n, n), 0)
    c = lax.broadcasted_iota(jnp.int32, (n, n), 1)
    return jnp.where(r > c, 1.0, 0.0).astype(BF16)


NEG_BIG = -1e30
LOG2E = 1.4426950408889634


def _sb_logs(zr, valid):
    scale = HEAD_DIM ** -0.5
    nz = zr * (-scale)
    tail = jnp.log(1.0 + jnp.exp2(jnp.abs(zr) * (-scale * LOG2E)))
    log_keep = jnp.minimum(nz, 0.0) - tail
    log_beta = log_keep - nz
    if valid is not None:
        log_keep = jnp.where(valid, log_keep, 0.0)
        log_beta = jnp.where(valid, log_beta, NEG_BIG)
    return log_keep.astype(BF16), log_beta, log_keep[:, 0:1]


def _attn_prompt_kernel(q_ref, k_ref, v_ref, o_ref, *scratch, tq, tk, rc, hb):
    z_ref, keep_ref, logb_ref, first_ref, later_ref, logw_ref, w_ref, acc_ref, carry_ref = scratch
    qi = pl.program_id(1)
    per = tq // tk
    nb = (qi + 1) * per
    u = _suffix_matrix(tk)
    chunks = [slice(r0, r0 + rc) for r0 in range(0, tq, rc)]

    def keys(ref, g, n):
        return ref[g, pl.ds(pl.multiple_of((nb - 1 - n) * tk, tk), tk), :]

    def raw_scores(g, n, slot):
        z_ref[g, slot] = lax.dot_general(q_ref[g], keys(k_ref, g, n), (((1,), (1,)), ((), ())),
                                         preferred_element_type=F32)

    def log_terms(g, slot, masked_from):
        for rows in chunks:
            valid = None
            if masked_from is not None:
                row = lax.broadcasted_iota(jnp.int32, (rc, tk), 0) + rows.start
                col = lax.broadcasted_iota(jnp.int32, (rc, tk), 1) + masked_from
                valid = col < row
            log_keep, log_beta, first = _sb_logs(z_ref[g, slot, rows, :], valid)
            keep_ref[g, slot, rows, :] = log_keep
            logb_ref[g, slot, rows, :] = log_beta
            first_ref[g, slot, rows, :] = first

    def suffix_sums(g, slot):
        later_ref[g, slot] = jnp.dot(keep_ref[g, slot], u, preferred_element_type=F32)
        for rows in chunks:
            later = later_ref[g, slot, rows, :]
            carry = carry_ref[g, rows, :]
            logw_ref[g, slot, rows, :] = logb_ref[g, slot, rows, :] + later + carry
            carry_ref[g, rows, :] = carry + later[:, 0:1] + first_ref[g, slot, rows, :]

    def values(g, n, slot):
        for rows in chunks:
            w_ref[g, slot, rows, :] = jnp.exp(logw_ref[g, slot, rows, :]).astype(BF16)
        acc_ref[g] += jnp.dot(w_ref[g, slot], keys(v_ref, g, n), preferred_element_type=F32)

    def step(t, parity, first_stage=0, last_stage=3, masked_from=None):
        for g in range(hb):
            if first_stage <= 0 <= last_stage:
                raw_scores(g, t, parity)
            if first_stage <= 2 <= last_stage:
                suffix_sums(g, parity)
            if first_stage <= 3 <= last_stage:
                values(g, t - 3, 1 - parity)
            if first_stage <= 1 <= last_stage:
                log_terms(g, 1 - parity, masked_from)

    carry_ref[...] = jnp.zeros_like(carry_ref)
    acc_ref[...] = jnp.zeros_like(acc_ref)

    @pl.when(qi == 0)
    def _():
        for g in range(hb):
            for n in range(per):
                raw_scores(g, n, 0)
                log_terms(g, 0, (per - 1 - n) * tk)
                suffix_sums(g, 0)
                values(g, n, 0)

    @pl.when(qi > 0)
    def _():
        step(0, 0, last_stage=0)
        step(1, 1, last_stage=1, masked_from=(per - 1) * tk)
        step(2, 0, last_stage=2, masked_from=(per - 2) * tk)
        step(3, 1)

        def body(m, _):
            step(4 + 2 * m, 0)
            step(5 + 2 * m, 1)
            return 0

        lax.fori_loop(0, (nb - 4) // 2, body, 0)
        step(nb, 0, first_stage=1)
        step(nb + 1, 1, first_stage=2)
        step(nb + 2, 0, first_stage=3)

    for g in range(hb):
        o_ref[:, g * HEAD_DIM:(g + 1) * HEAD_DIM] = acc_ref[g].astype(o_ref.dtype)


def _attn_prompt(qh, kh, vh, tq, tk, rc, hb):
    h, t, hd = qh.shape
    assert tq == 2 * tk and tq % rc == 0
    assert h % hb == 0
    resident = pl.BlockSpec((hb, t, hd), lambda hi, qi: (hi, 0, 0))
    return pl.pallas_call(
        functools.partial(_attn_prompt_kernel, tq=tq, tk=tk, rc=rc, hb=hb),
        grid=(h // hb, t // tq),
        in_specs=[pl.BlockSpec((hb, tq, hd), lambda hi, qi: (hi, qi, 0)), resident, resident],
        out_specs=pl.BlockSpec((tq, hb * hd), lambda hi, qi: (qi, hi)),
        out_shape=jax.ShapeDtypeStruct((t, h * hd), BF16),
        scratch_shapes=[
            pltpu.VMEM((hb, 2, tq, tk), F32),
            pltpu.VMEM((hb, 2, tq, tk), BF16),
            pltpu.VMEM((hb, 2, tq, tk), F32),
            pltpu.VMEM((hb, 2, tq, 1), F32),
            pltpu.VMEM((hb, 2, tq, tk), F32),
            pltpu.VMEM((hb, 2, tq, tk), F32),
            pltpu.VMEM((hb, 2, tq, tk), BF16),
            pltpu.VMEM((hb, tq, hd), F32),
            pltpu.VMEM((hb, tq, 1), F32),
        ],
        compiler_params=_params("parallel", "arbitrary"),
        name="attn_prompt",
    )(qh, kh, vh)


def _attn_sample_kernel(q_ref, kn_ref, vn_ref, ck_ref, cv_ref, o_ref,
                        kt_ref, vt_ref, z_ref, keep_ref, logb_ref, later_ref, w_ref, acc_ref,
                        carry_ref, *, s, tp, tk, pad, rc):
    j = pl.program_id(1)
    n_heads = q_ref.shape[0]
    m = n_heads * s
    chunks = [slice(r0, r0 + rc) for r0 in range(0, m, rc)]
    head_rows = [slice(h * s, (h + 1) * s) for h in range(n_heads)]
    nt = (((1,), (1,)), ((), ()))

    def attend(n, nk, keys_of, values_of, u, masked):
        for h in range(n_heads):
            z_ref[head_rows[h], 0:n] = lax.dot_general(q_ref[h], keys_of(h), nt,
                                                      preferred_element_type=F32)
        for sub in range(n // nk - 1, -1, -1):
            cols = slice(sub * nk, (sub + 1) * nk)
            for rows in chunks:
                valid = None
                if masked:
                    row = lax.broadcasted_iota(jnp.int32, (rc, nk), 0) % s
                    valid = lax.broadcasted_iota(jnp.int32, (rc, nk), 1) < row
                log_keep, log_beta, first = _sb_logs(z_ref[rows, cols], valid)
                keep_ref[rows, 0:nk] = log_keep
                logb_ref[rows, 0:nk] = log_beta
                carry_ref[1, rows, :] = first
            later_ref[:, 0:nk] = jnp.dot(keep_ref[:, 0:nk], u, preferred_element_type=F32)
            for rows in chunks:
                later = later_ref[rows, 0:nk]
                carry = carry_ref[0, rows, :]
                w_ref[rows, cols] = jnp.exp(logb_ref[rows, 0:nk] + later + carry).astype(BF16)
                carry_ref[0, rows, :] = carry + later[:, 0:1] + carry_ref[1, rows, :]
        for h in range(n_heads):
            acc_ref[head_rows[h], :] += jnp.dot(w_ref[head_rows[h], 0:n], values_of(h),
                                                preferred_element_type=F32)

    @pl.when(j == 0)
    def _():
        carry_ref[...] = jnp.zeros_like(carry_ref)
        acc_ref[...] = jnp.zeros_like(acc_ref)
        zeros = jnp.zeros((pad - s, HEAD_DIM), BF16)
        attend(pad, pad,
               lambda h: jnp.concatenate([kn_ref[h], zeros], axis=0),
               lambda h: jnp.concatenate([vn_ref[h], zeros], axis=0),
               _suffix_matrix(pad), True)

    kc = 16
    for c in range(0, tp, kc):
        kt_ref[:, c:c + kc, :] = jnp.swapaxes(ck_ref[c:c + kc], 0, 1).astype(BF16)
        vt_ref[:, c:c + kc, :] = jnp.swapaxes(cv_ref[c:c + kc], 0, 1).astype(BF16)
    attend(tp, tk, lambda h: kt_ref[h], lambda h: vt_ref[h], _suffix_matrix(tk), False)

    @pl.when(j == pl.num_programs(1) - 1)
    def _():
        for h in range(n_heads):
            o_ref[:, h * HEAD_DIM:(h + 1) * HEAD_DIM] = acc_ref[head_rows[h], :].astype(o_ref.dtype)


def _attn_sample(qh, kh, vh, cache_k, cache_v, s, tp, tk, rc):
    h, t, hd = qh.shape
    bn, past = cache_k.shape[:2]
    nkb = past // tp
    m = h * s
    assert rc % s == 0 and m % rc == 0 and tp % tk == 0 and hd <= tk
    new = pl.BlockSpec((h, s, hd), lambda b, j: (0, b, 0))
    old = pl.BlockSpec((None, tp, h, hd), lambda b, j: (b, nkb - 1 - j, 0, 0))
    return pl.pallas_call(
        functools.partial(_attn_sample_kernel, s=s, tp=tp, tk=tk, pad=hd, rc=rc),
        grid=(bn, nkb),
        in_specs=[new, new, new, old, old],
        out_specs=pl.BlockSpec((s, h * hd), lambda b, j: (b, 0)),
        out_shape=jax.ShapeDtypeStruct((t, h * hd), BF16),
        scratch_shapes=[
            pltpu.VMEM((h, tp, hd), BF16),
            pltpu.VMEM((h, tp, hd), BF16),
            pltpu.VMEM((m, tp), F32),
            pltpu.VMEM((m, tk), BF16),
            pltpu.VMEM((m, tk), F32),
            pltpu.VMEM((m, tk), F32),
            pltpu.VMEM((m, tp), BF16),
            pltpu.VMEM((m, hd), F32),
            pltpu.VMEM((2, m, 1), F32),
        ],
        compiler_params=_params("parallel", "arbitrary"),
        name="attn_sample",
    )(qh, kh, vh, cache_k, cache_v)


def _trunk(x, conv_buf, h0, pos0, cache, p, tm, ts):
    bn, s, d = x.shape
    t = bn * s
    x0 = x.reshape(t, d)

    gate, rec = _in_proj(x0, p["a_norm"], p["a_w_in"], tm, 512)
    hg, h_last, new_buf = _rglru(
        rec.reshape(bn, s, d), gate.reshape(bn, s, d), conv_buf, h0.reshape(bn, 1, d),
        p["a_conv_w"], p["a_conv_b"], p["a_b_r"], p["a_b_i"], p["a_lambda"], p["a_w_r"], p["a_w_i"],
        pos0, ts, 512)
    x1 = _matmul_res(hg.reshape(t, d), p["a_w_out"], x0, tm, 1024)
    x2 = _mlp(x1, p["mlp_norm0"], p["mlp_w_up0"], p["mlp_w_down0"], tm, 512)

    k, v, kh, vh, qh = _kvq(x2, p["kv_norm"], p["b_norm"], p["w_kv"], p["b_w_q"],
                            p["k_norm"], p["b_q_norm"], tm, 512)
    if cache is None:
        o = _attn_prompt(qh, kh, vh, 512, 256, 32, 2)
    else:
        o = _attn_sample(qh, kh, vh, cache[0], cache[1], s, 512, 256, 32)
    x3 = _matmul_res(o, p["b_w_o"], x2, tm, 1024)
    y = _mlp(x3, p["mlp_norm1"], p["mlp_w_up1"], p["mlp_w_down1"], tm, 512)

    return (y.reshape(bn, s, d), h_last.reshape(1, bn, d), new_buf.reshape(1, bn, CONV_W - 1, d),
            k.reshape(bn, s, N_HEADS, HEAD_DIM), v.reshape(bn, s, N_HEADS, HEAD_DIM))


def kernel(x_prompt, x_sample, state_lru_h, state_conv, cache_k, cache_v, a_norm, a_w_in, a_conv_w, a_conv_b, a_w_r, a_b_r, a_w_i, a_b_i, a_lambda, a_w_out, kv_norm, w_kv, k_norm, b_norm, b_w_q, b_q_norm, b_w_o, mlp_norm, mlp_w_up, mlp_w_down):
    row = lambda a: a.reshape(1, -1)
    p = {
        "a_norm": row(a_norm[0]), "a_w_in": a_w_in[0].astype(BF16),
        "a_conv_w": a_conv_w[0], "a_conv_b": row(a_conv_b[0]),
        "a_w_r": a_w_r[0].astype(BF16), "a_b_r": row(a_b_r[0]),
        "a_w_i": a_w_i[0].astype(BF16), "a_b_i": row(a_b_i[0]),
        "a_lambda": row(a_lambda[0]), "a_w_out": a_w_out[0].astype(BF16),
        "kv_norm": row(kv_norm), "w_kv": w_kv.astype(BF16), "k_norm": row(k_norm),
        "b_norm": row(b_norm[0]), "b_w_q": b_w_q[0].astype(BF16), "b_q_norm": row(b_q_norm[0]),
        "b_w_o": b_w_o[0].astype(BF16),
        "mlp_norm0": row(mlp_norm[0]), "mlp_w_up0": mlp_w_up[0].astype(BF16),
        "mlp_w_down0": mlp_w_down[0].astype(BF16),
        "mlp_norm1": row(mlp_norm[1]), "mlp_w_up1": mlp_w_up[1].astype(BF16),
        "mlp_w_down1": mlp_w_down[1].astype(BF16),
    }
    bp, sp, d = x_prompt.shape
    bs, ss, _ = x_sample.shape
    zero_conv = jnp.zeros((bp, CONV_W - 1, D_RNN), x_prompt.dtype)
    zero_h = jnp.zeros((bp, D_RNN), x_prompt.dtype)
    y_p, p_h, p_conv, p_k, p_v = _trunk(x_prompt, zero_conv, zero_h, 0, None, p, 512, 256)
    cache = (cache_k, cache_v)
    y_s, s_h, s_conv, s_k, s_v = _trunk(x_sample, state_conv[0], state_lru_h[0], PAST_LEN, cache,
                                        p, 512, ss)
    return (y_p, y_s, p_h, p_conv, p_k, p_v, s_h, s_conv, s_k, s_v)
```

```python
import functools

import jax
import jax.numpy as jnp
from jax import lax
from jax.experimental import pallas as pl
from jax.experimental.pallas import tpu as pltpu

F32 = jnp.float32
BF16 = jnp.bfloat16

D_MODEL = 2048
D_RNN = D_MODEL
N_HEADS = 16
HEAD_DIM = D_MODEL // N_HEADS
N_LRU_HEADS = 16
LRU_BLOCK = D_RNN // N_LRU_HEADS
CONV_W = 4
LRU_C = 8.0
D_FF = 4 * D_MODEL
PAST_LEN = 2048
EPS = 1e-6

V7X_VMEM_LIMIT_BYTES = 56 * 1024 * 1024
SUBLANES = 8


def _params(*sem):
    return pltpu.CompilerParams(dimension_semantics=sem, vmem_limit_bytes=V7X_VMEM_LIMIT_BYTES)


def _rmsnorm(x, g):
    y = x * lax.rsqrt(jnp.mean(x * x, axis=-1, keepdims=True) + EPS)
    return y * g


def _softplus(x):
    return jnp.maximum(x, 0.0) + jnp.log(1.0 + jnp.exp(-jnp.abs(x)))


def _head_rmsnorm(t, g):
    outs = []
    for k in range(t.shape[1] // HEAD_DIM):
        outs.append(_rmsnorm(t[:, k * HEAD_DIM:(k + 1) * HEAD_DIM], g))
    return jnp.concatenate(outs, axis=1) if len(outs) > 1 else outs[0]


def _in_proj_kernel(x_ref, g_ref, wg_ref, wr_ref, gate_ref, rec_ref, xn_ref):
    @pl.when(pl.program_id(1) == 0)
    def _():
        xn_ref[...] = _rmsnorm(x_ref[...], g_ref[...]).astype(BF16)

    xn = xn_ref[...]
    gate_ref[...] = jax.nn.gelu(jnp.dot(xn, wg_ref[...], preferred_element_type=F32))
    rec_ref[...] = jnp.dot(xn, wr_ref[...], preferred_element_type=F32)


def _in_proj(x, g, w, tm, tn):
    t, d = x.shape
    nh = D_RNN // tn
    return pl.pallas_call(
        _in_proj_kernel,
        grid=(t // tm, nh),
        in_specs=[
            pl.BlockSpec((tm, d), lambda i, j: (i, 0)),
            pl.BlockSpec((1, d), lambda i, j: (0, 0)),
            pl.BlockSpec((d, tn), lambda i, j: (0, j)),
            pl.BlockSpec((d, tn), lambda i, j: (0, j + nh)),
        ],
        out_specs=[
            pl.BlockSpec((tm, tn), lambda i, j: (i, j)),
            pl.BlockSpec((tm, tn), lambda i, j: (i, j)),
        ],
        out_shape=[jax.ShapeDtypeStruct((t, D_RNN), F32)] * 2,
        scratch_shapes=[pltpu.VMEM((tm, d), BF16)],
        compiler_params=_params("parallel", "arbitrary"),
        name="in_proj",
    )(x, g, w, w)


def _rglru_kernel(rec_ref, gate_ref, cbuf_ref, h0_ref, cw_ref, cb_ref, br_ref, bi_ref, lam_ref,
                  wr_ref, wi_ref, hg_ref, hlast_ref, nbuf_ref, ext_ref, hc_ref, *, ts, tc, pos0):
    t = pl.program_id(2)
    pad = SUBLANES

    @pl.when(t == 0)
    def _():
        ext_ref[pad - (CONV_W - 1):pad, :] = cbuf_ref[...]
        hc_ref[...] = h0_ref[...]

    ext_ref[pad:pad + ts, :] = rec_ref[...]
    c = cb_ref[...] + ext_ref[pad - 3:pad - 3 + ts, :] * cw_ref[0:1, :]
    for j in range(1, CONV_W):
        c = c + ext_ref[pad - 3 + j:pad - 3 + j + ts, :] * cw_ref[j:j + 1, :]

    c16 = c.astype(BF16)
    rs, is_ = [], []
    for k in range(tc // LRU_BLOCK):
        blk = c16[:, k * LRU_BLOCK:(k + 1) * LRU_BLOCK]
        rs.append(jnp.dot(blk, wr_ref[k], preferred_element_type=F32))
        is_.append(jnp.dot(blk, wi_ref[k], preferred_element_type=F32))
    r = jax.nn.sigmoid(jnp.concatenate(rs, axis=1) + br_ref[...])
    ig = jax.nn.sigmoid(jnp.concatenate(is_, axis=1) + bi_ref[...])

    log_a = (-LRU_C * r) * _softplus(-lam_ref[...])
    a = jnp.exp(log_a)
    m2 = jnp.tanh(-log_a) * (a * a + 1.0)
    mult = m2 * lax.rsqrt(jnp.maximum(m2, jnp.finfo(F32).tiny))
    row = lax.broadcasted_iota(jnp.int32, (ts, tc), 0)
    if pos0 == 0:
        mult = jnp.where((row == 0) & (t == 0), 1.0, mult)
    b = mult * ig * c

    groups = ts // SUBLANES
    a = a.reshape(groups, SUBLANES, tc)
    b = b.reshape(groups, SUBLANES, tc)
    sub = lax.broadcasted_iota(jnp.int32, (groups, SUBLANES, tc), 1)
    d = 1
    while d < SUBLANES:
        keep = sub >= d
        a_sh = jnp.where(keep, pltpu.roll(a, d, 1), 1.0)
        b_sh = jnp.where(keep, pltpu.roll(b, d, 1), 0.0)
        b = a * b_sh + b
        a = a * a_sh
        d *= 2
    state = hc_ref[...]
    for g in range(0, groups, 2):
        h0 = b[g] + a[g] * state
        h1 = b[g + 1] + a[g + 1] * h0[SUBLANES - 1:SUBLANES, :]
        state = h1[SUBLANES - 1:SUBLANES, :]
        rows = slice(g * SUBLANES, (g + 2) * SUBLANES)
        h = jnp.concatenate([h0, h1], axis=0)
        hg_ref[rows, :] = (h * gate_ref[rows, :]).astype(hg_ref.dtype)

    hc_ref[...] = state
    hlast_ref[...] = state
    nbuf_ref[...] = ext_ref[pad + ts - (CONV_W - 1):pad + ts, :]
    ext_ref[0:pad, :] = ext_ref[ts:ts + pad, :]


def _rglru(rec, gate, cbuf, h0, cw, cb, br, bi, lam, wr, wi, pos0, ts, tc):
    bn, s, c = rec.shape
    nblk = tc // LRU_BLOCK
    seq = pl.BlockSpec((None, ts, tc), lambda b, ci, ti: (b, ti, ci))
    vec = pl.BlockSpec((1, tc), lambda b, ci, ti: (0, ci))
    return pl.pallas_call(
        functools.partial(_rglru_kernel, ts=ts, tc=tc, pos0=pos0),
        grid=(bn, c // tc, s // ts),
        in_specs=[
            seq, seq,
            pl.BlockSpec((None, CONV_W - 1, tc), lambda b, ci, ti: (b, 0, ci)),
            pl.BlockSpec((None, 1, tc), lambda b, ci, ti: (b, 0, ci)),
            pl.BlockSpec((CONV_W, tc), lambda b, ci, ti: (0, ci)),
            vec, vec, vec, vec,
            pl.BlockSpec((nblk, LRU_BLOCK, LRU_BLOCK), lambda b, ci, ti: (ci, 0, 0)),
            pl.BlockSpec((nblk, LRU_BLOCK, LRU_BLOCK), lambda b, ci, ti: (ci, 0, 0)),
        ],
        out_specs=[
            seq,
            pl.BlockSpec((None, 1, tc), lambda b, ci, ti: (b, 0, ci)),
            pl.BlockSpec((None, CONV_W - 1, tc), lambda b, ci, ti: (b, 0, ci)),
        ],
        out_shape=[
            jax.ShapeDtypeStruct((bn, s, c), BF16),
            jax.ShapeDtypeStruct((bn, 1, c), F32),
            jax.ShapeDtypeStruct((bn, CONV_W - 1, c), F32),
        ],
        scratch_shapes=[pltpu.VMEM((ts + 2 * SUBLANES, tc), F32), pltpu.VMEM((1, tc), F32)],
        compiler_params=_params("parallel", "parallel", "arbitrary"),
        name="rglru",
    )(rec, gate, cbuf, h0, cw, cb, br, bi, lam, wr, wi)


def _matmul_res_kernel(a_ref, w_ref, res_ref, o_ref):
    o_ref[...] = res_ref[...] + jnp.dot(a_ref[...], w_ref[...], preferred_element_type=F32)


def _matmul_res(a, w, res, tm, tn):
    t, k = a.shape
    n = w.shape[1]
    return pl.pallas_call(
        _matmul_res_kernel,
        grid=(t // tm, n // tn),
        in_specs=[
            pl.BlockSpec((tm, k), lambda i, j: (i, 0)),
            pl.BlockSpec((k, tn), lambda i, j: (0, j)),
            pl.BlockSpec((tm, tn), lambda i, j: (i, j)),
        ],
        out_specs=pl.BlockSpec((tm, tn), lambda i, j: (i, j)),
        out_shape=jax.ShapeDtypeStruct((t, n), F32),
        compiler_params=_params("parallel", "arbitrary"),
        name="matmul_res",
    )(a, w, res)


def _mlp_kernel(x_ref, g_ref, wu_ref, wd_ref, o_ref, xn_ref):
    @pl.when(pl.program_id(1) == 0)
    def _():
        x = x_ref[...]
        xn_ref[...] = _rmsnorm(x, g_ref[...]).astype(BF16)
        o_ref[...] = x

    h = jnp.maximum(jnp.dot(xn_ref[...], wu_ref[...], preferred_element_type=F32), 0.0)
    o_ref[...] += jnp.dot((h * h).astype(BF16), wd_ref[...], preferred_element_type=F32)


def _mlp(x, g, wu, wd, tm, tf):
    t, d = x.shape
    f = wu.shape[1]
    return pl.pallas_call(
        _mlp_kernel,
        grid=(t // tm, f // tf),
        in_specs=[
            pl.BlockSpec((tm, d), lambda i, j: (i, 0)),
            pl.BlockSpec((1, d), lambda i, j: (0, 0)),
            pl.BlockSpec((d, tf), lambda i, j: (0, j)),
            pl.BlockSpec((tf, d), lambda i, j: (j, 0)),
        ],
        out_specs=pl.BlockSpec((tm, d), lambda i, j: (i, 0)),
        out_shape=jax.ShapeDtypeStruct((t, d), F32),
        scratch_shapes=[pltpu.VMEM((tm, d), BF16)],
        compiler_params=_params("parallel", "arbitrary"),
        name="mlp",
    )(x, g, wu, wd)


def _kvq_kernel(x_ref, gkv_ref, gq_ref, wk_ref, wv_ref, wq_ref, kn_ref, qn_ref,
                k_ref, v_ref, kh_ref, vh_ref, qh_ref, xkv_ref, xq_ref):
    @pl.when(pl.program_id(1) == 0)
    def _():
        x = x_ref[...]
        xhat = x * lax.rsqrt(jnp.mean(x * x, axis=-1, keepdims=True) + EPS)
        xkv_ref[...] = (xhat * gkv_ref[...]).astype(BF16)
        xq_ref[...] = (xhat * gq_ref[...]).astype(BF16)

    xkv = xkv_ref[...]
    k = _head_rmsnorm(jnp.dot(xkv, wk_ref[...], preferred_element_type=F32), kn_ref[...])
    v = jnp.dot(xkv, wv_ref[...], preferred_element_type=F32)
    q = _head_rmsnorm(jnp.dot(xq_ref[...], wq_ref[...], preferred_element_type=F32), qn_ref[...])
    k_ref[...] = k
    v_ref[...] = v
    for hh in range(kh_ref.shape[0]):
        sl = slice(hh * HEAD_DIM, (hh + 1) * HEAD_DIM)
        kh_ref[hh] = k[:, sl].astype(BF16)
        vh_ref[hh] = v[:, sl].astype(BF16)
        qh_ref[hh] = q[:, sl].astype(BF16)


def _kvq(x, gkv, gq, wkv, wq, kn, qn, tm, tn):
    t, d = x.shape
    nh = D_MODEL // tn
    hb = tn // HEAD_DIM
    tile = pl.BlockSpec((tm, tn), lambda i, j: (i, j))
    heads = pl.BlockSpec((hb, tm, HEAD_DIM), lambda i, j: (j, i, 0))
    vec = pl.BlockSpec((1, d), lambda i, j: (0, 0))
    hvec = pl.BlockSpec((1, HEAD_DIM), lambda i, j: (0, 0))
    return pl.pallas_call(
        _kvq_kernel,
        grid=(t // tm, nh),
        in_specs=[
            pl.BlockSpec((tm, d), lambda i, j: (i, 0)),
            vec, vec,
            pl.BlockSpec((d, tn), lambda i, j: (0, j)),
            pl.BlockSpec((d, tn), lambda i, j: (0, j + nh)),
            pl.BlockSpec((d, tn), lambda i, j: (0, j)),
            hvec, hvec,
        ],
        out_specs=[tile, tile, heads, heads, heads],
        out_shape=[
            jax.ShapeDtypeStruct((t, D_MODEL), F32),
            jax.ShapeDtypeStruct((t, D_MODEL), F32),
            jax.ShapeDtypeStruct((N_HEADS, t, HEAD_DIM), BF16),
            jax.ShapeDtypeStruct((N_HEADS, t, HEAD_DIM), BF16),
            jax.ShapeDtypeStruct((N_HEADS, t, HEAD_DIM), BF16),
        ],
        scratch_shapes=[pltpu.VMEM((tm, d), BF16), pltpu.VMEM((tm, d), BF16)],
        compiler_params=_params("parallel", "arbitrary"),
        name="kvq",
    )(x, gkv, gq, wkv, wkv, wq, kn, qn)


def _suffix_matrix(n):
    r = lax.broadcasted_iota(jnp.int32, (n, n), 0)
    c = lax.broadcasted_iota(jnp.int32, (n, n), 1)
    return jnp.where(r > c, 1.0, 0.0).astype(BF16)


NEG_BIG = -1e30
STICK_GONE = -110.0
LOG2E = 1.4426950408889634


def _sb_logs(zr, valid):
    scale = HEAD_DIM ** -0.5
    nz = zr * (-scale)
    tail = jnp.log(1.0 + jnp.exp2(jnp.abs(zr) * (-scale * LOG2E)))
    log_keep = jnp.minimum(nz, 0.0) - tail
    log_beta = log_keep - nz
    if valid is not None:
        log_keep = jnp.where(valid, log_keep, 0.0)
        log_beta = jnp.where(valid, log_beta, NEG_BIG)
    return log_keep.astype(BF16), log_beta, log_keep[:, 0:1]


def _attn_prompt_kernel(q_ref, k_ref, v_ref, o_ref, *scratch, tq, tk, rc, hb):
    z_ref, keep_ref, logb_ref, first_ref, later_ref, logw_ref, w_ref, acc_ref, carry_ref = scratch
    qi = pl.program_id(1)
    per = tq // tk
    nb = (qi + 1) * per
    u = _suffix_matrix(tk)
    chunks = [slice(r0, r0 + rc) for r0 in range(0, tq, rc)]

    def keys(ref, g, n):
        return ref[g, pl.ds(pl.multiple_of((nb - 1 - n) * tk, tk), tk), :]

    def raw_scores(g, n, slot):
        z_ref[g, slot] = lax.dot_general(q_ref[g], keys(k_ref, g, n), (((1,), (1,)), ((), ())),
                                         preferred_element_type=F32)

    def log_terms(g, slot, masked_from):
        for rows in chunks:
            valid = None
            if masked_from is not None:
                row = lax.broadcasted_iota(jnp.int32, (rc, tk), 0) + rows.start
                col = lax.broadcasted_iota(jnp.int32, (rc, tk), 1) + masked_from
                valid = col < row
            log_keep, log_beta, first = _sb_logs(z_ref[g, slot, rows, :], valid)
            keep_ref[g, slot, rows, :] = log_keep
            logb_ref[g, slot, rows, :] = log_beta
            first_ref[g, slot, rows, :] = first

    def suffix_sums(g, slot):
        later_ref[g, slot] = jnp.dot(keep_ref[g, slot], u, preferred_element_type=F32)
        for rows in chunks:
            later = later_ref[g, slot, rows, :]
            carry = carry_ref[g, rows, :]
            logw_ref[g, slot, rows, :] = logb_ref[g, slot, rows, :] + later + carry
            carry_ref[g, rows, :] = carry + later[:, 0:1] + first_ref[g, slot, rows, :]

    def values(g, n, slot):
        for rows in chunks:
            w_ref[g, slot, rows, :] = jnp.exp(logw_ref[g, slot, rows, :]).astype(BF16)
        acc_ref[g] += jnp.dot(w_ref[g, slot], keys(v_ref, g, n), preferred_element_type=F32)

    def step(t, parity, first_stage=0, last_stage=3, masked_from=None):
        for g in range(hb):
            if first_stage <= 0 <= last_stage:
                raw_scores(g, t, parity)
            if first_stage <= 2 <= last_stage:
                suffix_sums(g, parity)
            if first_stage <= 3 <= last_stage:
                values(g, t - 3, 1 - parity)
            if first_stage <= 1 <= last_stage:
                log_terms(g, 1 - parity, masked_from)

    carry_ref[...] = jnp.zeros_like(carry_ref)
    acc_ref[...] = jnp.zeros_like(acc_ref)

    @pl.when(qi == 0)
    def _():
        for g in range(hb):
            for n in range(per):
                raw_scores(g, n, 0)
                log_terms(g, 0, (per - 1 - n) * tk)
                suffix_sums(g, 0)
                values(g, n, 0)

    @pl.when(qi > 0)
    def _():
        step(0, 0, last_stage=0)
        step(1, 1, last_stage=1, masked_from=(per - 1) * tk)
        step(2, 0, last_stage=2, masked_from=(per - 2) * tk)
        step(3, 1)

        def stick_left():
            return jnp.max(carry_ref[...]) > STICK_GONE

        def body(state):
            m, _ = state
            step(4 + 2 * m, 0)
            step(5 + 2 * m, 1)
            return m + 1, stick_left()

        trips = (nb - 4) // 2
        m, _ = lax.while_loop(lambda state: (state[0] < trips) & state[1], body,
                              (jnp.int32(0), stick_left()))
        end = 4 + 2 * m
        step(end, 0, first_stage=1)
        step(end + 1, 1, first_stage=2)
        step(end + 2, 0, first_stage=3)

    for g in range(hb):
        o_ref[:, g * HEAD_DIM:(g + 1) * HEAD_DIM] = acc_ref[g].astype(o_ref.dtype)


def _attn_prompt(qh, kh, vh, tq, tk, rc, hb):
    h, t, hd = qh.shape
    assert tq == 2 * tk and tq % rc == 0
    assert h % hb == 0
    resident = pl.BlockSpec((hb, t, hd), lambda hi, qi: (hi, 0, 0))
    return pl.pallas_call(
        functools.partial(_attn_prompt_kernel, tq=tq, tk=tk, rc=rc, hb=hb),
        grid=(h // hb, t // tq),
        in_specs=[pl.BlockSpec((hb, tq, hd), lambda hi, qi: (hi, qi, 0)), resident, resident],
        out_specs=pl.BlockSpec((tq, hb * hd), lambda hi, qi: (qi, hi)),
        out_shape=jax.ShapeDtypeStruct((t, h * hd), BF16),
        scratch_shapes=[
            pltpu.VMEM((hb, 2, tq, tk), F32),
            pltpu.VMEM((hb, 2, tq, tk), BF16),
            pltpu.VMEM((hb, 2, tq, tk), F32),
            pltpu.VMEM((hb, 2, tq, 1), F32),
            pltpu.VMEM((hb, 2, tq, tk), F32),
            pltpu.VMEM((hb, 2, tq, tk), F32),
            pltpu.VMEM((hb, 2, tq, tk), BF16),
            pltpu.VMEM((hb, tq, hd), F32),
            pltpu.VMEM((hb, tq, 1), F32),
        ],
        compiler_params=_params("parallel", "arbitrary"),
        name="attn_prompt",
    )(qh, kh, vh)


def _attn_sample_kernel(q_ref, kn_ref, vn_ref, ck_ref, cv_ref, o_ref,
                        kt_ref, vt_ref, z_ref, keep_ref, logb_ref, later_ref, w_ref, acc_ref,
                        carry_ref, *, s, tp, tk, pad, rc):
    j = pl.program_id(1)
    n_heads = q_ref.shape[0]
    m = n_heads * s
    chunks = [slice(r0, r0 + rc) for r0 in range(0, m, rc)]
    head_rows = [slice(h * s, (h + 1) * s) for h in range(n_heads)]
    nt = (((1,), (1,)), ((), ()))

    def attend(n, nk, keys_of, values_of, u, masked):
        for h in range(n_heads):
            z_ref[head_rows[h], 0:n] = lax.dot_general(q_ref[h], keys_of(h), nt,
                                                      preferred_element_type=F32)
        for sub in range(n // nk - 1, -1, -1):
            cols = slice(sub * nk, (sub + 1) * nk)
            for rows in chunks:
                valid = None
                if masked:
                    row = lax.broadcasted_iota(jnp.int32, (rc, nk), 0) % s
                    valid = lax.broadcasted_iota(jnp.int32, (rc, nk), 1) < row
                log_keep, log_beta, first = _sb_logs(z_ref[rows, cols], valid)
                keep_ref[rows, 0:nk] = log_keep
                logb_ref[rows, 0:nk] = log_beta
                carry_ref[1, rows, :] = first
            later_ref[:, 0:nk] = jnp.dot(keep_ref[:, 0:nk], u, preferred_element_type=F32)
            for rows in chunks:
                later = later_ref[rows, 0:nk]
                carry = carry_ref[0, rows, :]
                w_ref[rows, cols] = jnp.exp(logb_ref[rows, 0:nk] + later + carry).astype(BF16)
                carry_ref[0, rows, :] = carry + later[:, 0:1] + carry_ref[1, rows, :]
        for h in range(n_heads):
            acc_ref[head_rows[h], :] += jnp.dot(w_ref[head_rows[h], 0:n], values_of(h),
                                                preferred_element_type=F32)

    @pl.when(j == 0)
    def _():
        carry_ref[...] = jnp.zeros_like(carry_ref)
        acc_ref[...] = jnp.zeros_like(acc_ref)
        zeros = jnp.zeros((pad - s, HEAD_DIM), BF16)
        attend(pad, pad,
               lambda h: jnp.concatenate([kn_ref[h], zeros], axis=0),
               lambda h: jnp.concatenate([vn_ref[h], zeros], axis=0),
               _suffix_matrix(pad), True)

    kc = 16
    for c in range(0, tp, kc):
        kt_ref[:, c:c + kc, :] = jnp.swapaxes(ck_ref[c:c + kc], 0, 1).astype(BF16)
        vt_ref[:, c:c + kc, :] = jnp.swapaxes(cv_ref[c:c + kc], 0, 1).astype(BF16)
    attend(tp, tk, lambda h: kt_ref[h], lambda h: vt_ref[h], _suffix_matrix(tk), False)

    @pl.when(j == pl.num_programs(1) - 1)
    def _():
        for h in range(n_heads):
            o_ref[:, h * HEAD_DIM:(h + 1) * HEAD_DIM] = acc_ref[head_rows[h], :].astype(o_ref.dtype)


def _attn_sample(qh, kh, vh, cache_k, cache_v, s, tp, tk, rc):
    h, t, hd = qh.shape
    bn, past = cache_k.shape[:2]
    nkb = past // tp
    m = h * s
    assert rc % s == 0 and m % rc == 0 and tp % tk == 0 and hd <= tk
    new = pl.BlockSpec((h, s, hd), lambda b, j: (0, b, 0))
    old = pl.BlockSpec((None, tp, h, hd), lambda b, j: (b, nkb - 1 - j, 0, 0))
    return pl.pallas_call(
        functools.partial(_attn_sample_kernel, s=s, tp=tp, tk=tk, pad=hd, rc=rc),
        grid=(bn, nkb),
        in_specs=[new, new, new, old, old],
        out_specs=pl.BlockSpec((s, h * hd), lambda b, j: (b, 0)),
        out_shape=jax.ShapeDtypeStruct((t, h * hd), BF16),
        scratch_shapes=[
            pltpu.VMEM((h, tp, hd), BF16),
            pltpu.VMEM((h, tp, hd), BF16),
            pltpu.VMEM((m, tp), F32),
            pltpu.VMEM((m, tk), BF16),
            pltpu.VMEM((m, tk), F32),
            pltpu.VMEM((m, tk), F32),
            pltpu.VMEM((m, tp), BF16),
            pltpu.VMEM((m, hd), F32),
            pltpu.VMEM((2, m, 1), F32),
        ],
        compiler_params=_params("parallel", "arbitrary"),
        name="attn_sample",
    )(qh, kh, vh, cache_k, cache_v)


def _trunk(x, conv_buf, h0, pos0, cache, p, tm, ts):
    bn, s, d = x.shape
    t = bn * s
    x0 = x.reshape(t, d)

    gate, rec = _in_proj(x0, p["a_norm"], p["a_w_in"], tm, 512)
    hg, h_last, new_buf = _rglru(
        rec.reshape(bn, s, d), gate.reshape(bn, s, d), conv_buf, h0.reshape(bn, 1, d),
        p["a_conv_w"], p["a_conv_b"], p["a_b_r"], p["a_b_i"], p["a_lambda"], p["a_w_r"], p["a_w_i"],
        pos0, ts, 512)
    x1 = _matmul_res(hg.reshape(t, d), p["a_w_out"], x0, tm, 1024)
    x2 = _mlp(x1, p["mlp_norm0"], p["mlp_w_up0"], p["mlp_w_down0"], tm, 512)

    k, v, kh, vh, qh = _kvq(x2, p["kv_norm"], p["b_norm"], p["w_kv"], p["b_w_q"],
                            p["k_norm"], p["b_q_norm"], tm, 512)
    if cache is None:
        o = _attn_prompt(qh, kh, vh, 512, 256, 32, 2)
    else:
        o = _attn_sample(qh, kh, vh, cache[0], cache[1], s, 512, 256, 32)
    x3 = _matmul_res(o, p["b_w_o"], x2, tm, 1024)
    y = _mlp(x3, p["mlp_norm1"], p["mlp_w_up1"], p["mlp_w_down1"], tm, 512)

    return (y.reshape(bn, s, d), h_last.reshape(1, bn, d), new_buf.reshape(1, bn, CONV_W - 1, d),
            k.reshape(bn, s, N_HEADS, HEAD_DIM), v.reshape(bn, s, N_HEADS, HEAD_DIM))


def kernel(x_prompt, x_sample, state_lru_h, state_conv, cache_k, cache_v, a_norm, a_w_in, a_conv_w, a_conv_b, a_w_r, a_b_r, a_w_i, a_b_i, a_lambda, a_w_out, kv_norm, w_kv, k_norm, b_norm, b_w_q, b_q_norm, b_w_o, mlp_norm, mlp_w_up, mlp_w_down):
    row = lambda a: a.reshape(1, -1)
    p = {
        "a_norm": row(a_norm[0]), "a_w_in": a_w_in[0].astype(BF16),
        "a_conv_w": a_conv_w[0], "a_conv_b": row(a_conv_b[0]),
        "a_w_r": a_w_r[0].astype(BF16), "a_b_r": row(a_b_r[0]),
        "a_w_i": a_w_i[0].astype(BF16), "a_b_i": row(a_b_i[0]),
        "a_lambda": row(a_lambda[0]), "a_w_out": a_w_out[0].astype(BF16),
        "kv_norm": row(kv_norm), "w_kv": w_kv.astype(BF16), "k_norm": row(k_norm),
        "b_norm": row(b_norm[0]), "b_w_q": b_w_q[0].astype(BF16), "b_q_norm": row(b_q_norm[0]),
        "b_w_o": b_w_o[0].astype(BF16),
        "mlp_norm0": row(mlp_norm[0]), "mlp_w_up0": mlp_w_up[0].astype(BF16),
        "mlp_w_down0": mlp_w_down[0].astype(BF16),
        "mlp_norm1": row(mlp_norm[1]), "mlp_w_up1": mlp_w_up[1].astype(BF16),
        "mlp_w_down1": mlp_w_down[1].astype(BF16),
    }
    bp, sp, d = x_prompt.shape
    bs, ss, _ = x_sample.shape
    zero_conv = jnp.zeros((bp, CONV_W - 1, D_RNN), x_prompt.dtype)
    zero_h = jnp.zeros((bp, D_RNN), x_prompt.dtype)
    y_p, p_h, p_conv, p_k, p_v = _trunk(x_prompt, zero_conv, zero_h, 0, None, p, 512, 256)
    cache = (cache_k, cache_v)
    y_s, s_h, s_conv, s_k, s_v = _trunk(x_sample, state_conv[0], state_lru_h[0], PAST_LEN, cache,
                                        p, 512, ss)
    return (y_p, y_s, p_h, p_conv, p_k, p_v, s_h, s_conv, s_k, s_v)
```

```python
import functools

import jax
import jax.numpy as jnp
from jax import lax
from jax.experimental import pallas as pl
from jax.experimental.pallas import tpu as pltpu

F32 = jnp.float32
BF16 = jnp.bfloat16

D_MODEL = 2048
D_RNN = D_MODEL
N_HEADS = 16
HEAD_DIM = D_MODEL // N_HEADS
N_LRU_HEADS = 16
LRU_BLOCK = D_RNN // N_LRU_HEADS
CONV_W = 4
LRU_C = 8.0
D_FF = 4 * D_MODEL
PAST_LEN = 2048
EPS = 1e-6

V7X_VMEM_LIMIT_BYTES = 56 * 1024 * 1024
SUBLANES = 8


def _params(*sem):
    return pltpu.CompilerParams(dimension_semantics=sem, vmem_limit_bytes=V7X_VMEM_LIMIT_BYTES)


def _rmsnorm(x, g):
    y = x * lax.rsqrt(jnp.mean(x * x, axis=-1, keepdims=True) + EPS)
    return y * g


def _softplus(x):
    return jnp.maximum(x, 0.0) + jnp.log(1.0 + jnp.exp(-jnp.abs(x)))


def _head_rmsnorm(t, g):
    outs = []
    for k in range(t.shape[1] // HEAD_DIM):
        outs.append(_rmsnorm(t[:, k * HEAD_DIM:(k + 1) * HEAD_DIM], g))
    return jnp.concatenate(outs, axis=1) if len(outs) > 1 else outs[0]


def _in_proj_kernel(x_ref, g_ref, wg_ref, wr_ref, gate_ref, rec_ref, xn_ref):
    @pl.when(pl.program_id(1) == 0)
    def _():
        xn_ref[...] = _rmsnorm(x_ref[...], g_ref[...]).astype(BF16)

    xn = xn_ref[...]
    gate_ref[...] = jax.nn.gelu(jnp.dot(xn, wg_ref[...], preferred_element_type=F32))
    rec_ref[...] = jnp.dot(xn, wr_ref[...], preferred_element_type=F32)


def _in_proj(x, g, w, tm, tn):
    t, d = x.shape
    nh = D_RNN // tn
    return pl.pallas_call(
        _in_proj_kernel,
        grid=(t // tm, nh),
        in_specs=[
            pl.BlockSpec((tm, d), lambda i, j: (i, 0)),
            pl.BlockSpec((1, d), lambda i, j: (0, 0)),
            pl.BlockSpec((d, tn), lambda i, j: (0, j)),
            pl.BlockSpec((d, tn), lambda i, j: (0, j + nh)),
        ],
        out_specs=[
            pl.BlockSpec((tm, tn), lambda i, j: (i, j)),
            pl.BlockSpec((tm, tn), lambda i, j: (i, j)),
        ],
        out_shape=[jax.ShapeDtypeStruct((t, D_RNN), F32)] * 2,
        scratch_shapes=[pltpu.VMEM((tm, d), BF16)],
        compiler_params=_params("parallel", "arbitrary"),
        name="in_proj",
    )(x, g, w, w)


def _rglru_kernel(rec_ref, gate_ref, cbuf_ref, h0_ref, cw_ref, cb_ref, br_ref, bi_ref, lam_ref,
                  wr_ref, wi_ref, hg_ref, hlast_ref, nbuf_ref, ext_ref, hc_ref, *, ts, tc, pos0):
    t = pl.program_id(2)
    pad = SUBLANES

    @pl.when(t == 0)
    def _():
        ext_ref[pad - (CONV_W - 1):pad, :] = cbuf_ref[...]
        hc_ref[...] = h0_ref[...]

    ext_ref[pad:pad + ts, :] = rec_ref[...]
    c = cb_ref[...] + ext_ref[pad - 3:pad - 3 + ts, :] * cw_ref[0:1, :]
    for j in range(1, CONV_W):
        c = c + ext_ref[pad - 3 + j:pad - 3 + j + ts, :] * cw_ref[j:j + 1, :]

    c16 = c.astype(BF16)
    rs, is_ = [], []
    for k in range(tc // LRU_BLOCK):
        blk = c16[:, k * LRU_BLOCK:(k + 1) * LRU_BLOCK]
        rs.append(jnp.dot(blk, wr_ref[k], preferred_element_type=F32))
        is_.append(jnp.dot(blk, wi_ref[k], preferred_element_type=F32))
    r = jax.nn.sigmoid(jnp.concatenate(rs, axis=1) + br_ref[...])
    ig = jax.nn.sigmoid(jnp.concatenate(is_, axis=1) + bi_ref[...])

    log_a = (-LRU_C * r) * _softplus(-lam_ref[...])
    a = jnp.exp(log_a)
    m2 = jnp.tanh(-log_a) * (a * a + 1.0)
    mult = m2 * lax.rsqrt(jnp.maximum(m2, jnp.finfo(F32).tiny))
    row = lax.broadcasted_iota(jnp.int32, (ts, tc), 0)
    if pos0 == 0:
        mult = jnp.where((row == 0) & (t == 0), 1.0, mult)
    b = mult * ig * c

    groups = ts // SUBLANES
    a = a.reshape(groups, SUBLANES, tc)
    b = b.reshape(groups, SUBLANES, tc)
    sub = lax.broadcasted_iota(jnp.int32, (groups, SUBLANES, tc), 1)
    d = 1
    while d < SUBLANES:
        keep = sub >= d
        a_sh = jnp.where(keep, pltpu.roll(a, d, 1), 1.0)
        b_sh = jnp.where(keep, pltpu.roll(b, d, 1), 0.0)
        b = a * b_sh + b
        a = a * a_sh
        d *= 2
    state = hc_ref[...]
    for g in range(0, groups, 2):
        h0 = b[g] + a[g] * state
        h1 = b[g + 1] + a[g + 1] * h0[SUBLANES - 1:SUBLANES, :]
        state = h1[SUBLANES - 1:SUBLANES, :]
        rows = slice(g * SUBLANES, (g + 2) * SUBLANES)
        h = jnp.concatenate([h0, h1], axis=0)
        hg_ref[rows, :] = (h * gate_ref[rows, :]).astype(hg_ref.dtype)

    hc_ref[...] = state
    hlast_ref[...] = state
    nbuf_ref[...] = ext_ref[pad + ts - (CONV_W - 1):pad + ts, :]
    ext_ref[0:pad, :] = ext_ref[ts:ts + pad, :]


def _rglru(rec, gate, cbuf, h0, cw, cb, br, bi, lam, wr, wi, pos0, ts, tc):
    bn, s, c = rec.shape
    nblk = tc // LRU_BLOCK
    seq = pl.BlockSpec((None, ts, tc), lambda b, ci, ti: (b, ti, ci))
    vec = pl.BlockSpec((1, tc), lambda b, ci, ti: (0, ci))
    return pl.pallas_call(
        functools.partial(_rglru_kernel, ts=ts, tc=tc, pos0=pos0),
        grid=(bn, c // tc, s // ts),
        in_specs=[
            seq, seq,
            pl.BlockSpec((None, CONV_W - 1, tc), lambda b, ci, ti: (b, 0, ci)),
            pl.BlockSpec((None, 1, tc), lambda b, ci, ti: (b, 0, ci)),
            pl.BlockSpec((CONV_W, tc), lambda b, ci, ti: (0, ci)),
            vec, vec, vec, vec,
            pl.BlockSpec((nblk, LRU_BLOCK, LRU_BLOCK), lambda b, ci, ti: (ci, 0, 0)),
            pl.BlockSpec((nblk, LRU_BLOCK, LRU_BLOCK), lambda b, ci, ti: (ci, 0, 0)),
        ],
        out_specs=[
            seq,
            pl.BlockSpec((None, 1, tc), lambda b, ci, ti: (b, 0, ci)),
            pl.BlockSpec((None, CONV_W - 1, tc), lambda b, ci, ti: (b, 0, ci)),
        ],
        out_shape=[
            jax.ShapeDtypeStruct((bn, s, c), BF16),
            jax.ShapeDtypeStruct((bn, 1, c), F32),
            jax.ShapeDtypeStruct((bn, CONV_W - 1, c), F32),
        ],
        scratch_shapes=[pltpu.VMEM((ts + 2 * SUBLANES, tc), F32), pltpu.VMEM((1, tc), F32)],
        compiler_params=_params("parallel", "parallel", "arbitrary"),
        name="rglru",
    )(rec, gate, cbuf, h0, cw, cb, br, bi, lam, wr, wi)


def _matmul_res_kernel(a_ref, w_ref, res_ref, o_ref):
    o_ref[...] = res_ref[...] + jnp.dot(a_ref[...], w_ref[...], preferred_element_type=F32)


def _matmul_res(a, w, res, tm, tn):
    t, k = a.shape
    n = w.shape[1]
    return pl.pallas_call(
        _matmul_res_kernel,
        grid=(t // tm, n // tn),
        in_specs=[
            pl.BlockSpec((tm, k), lambda i, j: (i, 0)),
            pl.BlockSpec((k, tn), lambda i, j: (0, j)),
            pl.BlockSpec((tm, tn), lambda i, j: (i, j)),
        ],
        out_specs=pl.BlockSpec((tm, tn), lambda i, j: (i, j)),
        out_shape=jax.ShapeDtypeStruct((t, n), F32),
        compiler_params=_params("parallel", "arbitrary"),
        name="matmul_res",
    )(a, w, res)


def _mlp_kernel(x_ref, g_ref, wu_ref, wd_ref, o_ref, xn_ref):
    @pl.when(pl.program_id(1) == 0)
    def _():
        x = x_ref[...]
        xn_ref[...] = _rmsnorm(x, g_ref[...]).astype(BF16)
        o_ref[...] = x

    h = jnp.maximum(jnp.dot(xn_ref[...], wu_ref[...], preferred_element_type=F32), 0.0)
    o_ref[...] += jnp.dot((h * h).astype(BF16), wd_ref[...], preferred_element_type=F32)


def _mlp(x, g, wu, wd, tm, tf):
    t, d = x.shape
    f = wu.shape[1]
    return pl.pallas_call(
        _mlp_kernel,
        grid=(t // tm, f // tf),
        in_specs=[
            pl.BlockSpec((tm, d), lambda i, j: (i, 0)),
            pl.BlockSpec((1, d), lambda i, j: (0, 0)),
            pl.BlockSpec((d, tf), lambda i, j: (0, j)),
            pl.BlockSpec((tf, d), lambda i, j: (j, 0)),
        ],
        out_specs=pl.BlockSpec((tm, d), lambda i, j: (i, 0)),
        out_shape=jax.ShapeDtypeStruct((t, d), F32),
        scratch_shapes=[pltpu.VMEM((tm, d), BF16)],
        compiler_params=_params("parallel", "arbitrary"),
        name="mlp",
    )(x, g, wu, wd)


def _kvq_kernel(x_ref, gkv_ref, gq_ref, wk_ref, wv_ref, wq_ref, kn_ref, qn_ref,
                k_ref, v_ref, kh_ref, vh_ref, qh_ref, xkv_ref, xq_ref):
    @pl.when(pl.program_id(1) == 0)
    def _():
        x = x_ref[...]
        xhat = x * lax.rsqrt(jnp.mean(x * x, axis=-1, keepdims=True) + EPS)
        xkv_ref[...] = (xhat * gkv_ref[...]).astype(BF16)
        xq_ref[...] = (xhat * gq_ref[...]).astype(BF16)

    xkv = xkv_ref[...]
    k = _head_rmsnorm(jnp.dot(xkv, wk_ref[...], preferred_element_type=F32), kn_ref[...])
    v = jnp.dot(xkv, wv_ref[...], preferred_element_type=F32)
    q = _head_rmsnorm(jnp.dot(xq_ref[...], wq_ref[...], preferred_element_type=F32), qn_ref[...])
    k_ref[...] = k
    v_ref[...] = v
    for hh in range(kh_ref.shape[0]):
        sl = slice(hh * HEAD_DIM, (hh + 1) * HEAD_DIM)
        kh_ref[hh] = k[:, sl].astype(BF16)
        vh_ref[hh] = v[:, sl].astype(BF16)
        qh_ref[hh] = q[:, sl].astype(BF16)


def _kvq(x, gkv, gq, wkv, wq, kn, qn, tm, tn):
    t, d = x.shape
    nh = D_MODEL // tn
    hb = tn // HEAD_DIM
    tile = pl.BlockSpec((tm, tn), lambda i, j: (i, j))
    heads = pl.BlockSpec((hb, tm, HEAD_DIM), lambda i, j: (j, i, 0))
    vec = pl.BlockSpec((1, d), lambda i, j: (0, 0))
    hvec = pl.BlockSpec((1, HEAD_DIM), lambda i, j: (0, 0))
    return pl.pallas_call(
        _kvq_kernel,
        grid=(t // tm, nh),
        in_specs=[
            pl.BlockSpec((tm, d), lambda i, j: (i, 0)),
            vec, vec,
            pl.BlockSpec((d, tn), lambda i, j: (0, j)),
            pl.BlockSpec((d, tn), lambda i, j: (0, j + nh)),
            pl.BlockSpec((d, tn), lambda i, j: (0, j)),
            hvec, hvec,
        ],
        out_specs=[tile, tile, heads, heads, heads],
        out_shape=[
            jax.ShapeDtypeStruct((t, D_MODEL), F32),
            jax.ShapeDtypeStruct((t, D_MODEL), F32),
            jax.ShapeDtypeStruct((N_HEADS, t, HEAD_DIM), BF16),
            jax.ShapeDtypeStruct((N_HEADS, t, HEAD_DIM), BF16),
            jax.ShapeDtypeStruct((N_HEADS, t, HEAD_DIM), BF16),
        ],
        scratch_shapes=[pltpu.VMEM((tm, d), BF16), pltpu.VMEM((tm, d), BF16)],
        compiler_params=_params("parallel", "arbitrary"),
        name="kvq",
    )(x, gkv, gq, wkv, wkv, wq, kn, qn)


def _suffix_matrix(n):
    r = lax.broadcasted_iota(jnp.int32, (n, n), 0)
    c = lax.broadcasted_iota(jnp.int32, (n, n), 1)
    return jnp.where(r > c, 1.0, 0.0).astype(BF16)


NEG_BIG = -1e30
STICK_GONE = -110.0
LOG2E = 1.4426950408889634


def _sb_logs(zr, valid):
    scale = HEAD_DIM ** -0.5
    nz = zr * (-scale)
    tail = jnp.log(1.0 + jnp.exp2(jnp.abs(zr) * (-scale * LOG2E)))
    log_keep = jnp.minimum(nz, 0.0) - tail
    log_beta = log_keep - nz
    if valid is not None:
        log_keep = jnp.where(valid, log_keep, 0.0)
        log_beta = jnp.where(valid, log_beta, NEG_BIG)
    return log_keep.astype(BF16), log_beta, log_keep[:, 0:1]


def _attn_rows_kernel(q_ref, k_ref, v_ref, o_ref, z_ref, keep_ref, logb_ref, first_ref, later_ref,
                      w_ref, acc_ref, carry_ref, *, tq, rc, hb):
    qi = pl.program_id(1)
    tk = tq
    row0 = qi * tq
    u = _suffix_matrix(tk)
    chunks = [slice(r0, r0 + rc) for r0 in range(0, tq, rc)]
    nt = (((1,), (1,)), ((), ()))

    def attend(g, key0, nblk, diagonal):
        n = nblk * tk
        ks = pl.ds(pl.multiple_of(key0, tk), n)
        z_ref[g, :, 0:n] = lax.dot_general(q_ref[g], k_ref[g, ks, :], nt, preferred_element_type=F32)
        for blk in range(nblk - 1, -1, -1):
            cols = slice(blk * tk, (blk + 1) * tk)
            for rows in chunks:
                valid = None
                if diagonal and blk == nblk - 1:
                    row = lax.broadcasted_iota(jnp.int32, (rc, tk), 0) + rows.start
                    valid = lax.broadcasted_iota(jnp.int32, (rc, tk), 1) < row
                log_keep, log_beta, first = _sb_logs(z_ref[g, rows, cols], valid)
                keep_ref[g, rows, cols] = log_keep
                logb_ref[g, rows, cols] = log_beta
                first_ref[g, blk, rows, :] = first
            later_ref[g, :, cols] = jnp.dot(keep_ref[g, :, cols], u, preferred_element_type=F32)
            for rows in chunks:
                later = later_ref[g, rows, cols]
                carry = carry_ref[g, rows, :]
                w_ref[g, rows, cols] = jnp.exp(logb_ref[g, rows, cols] + later + carry).astype(BF16)
                carry_ref[g, rows, :] = carry + later[:, 0:1] + first_ref[g, blk, rows, :]
        acc_ref[g] += jnp.dot(w_ref[g, :, 0:n], v_ref[g, ks, :], preferred_element_type=F32)

    carry_ref[...] = jnp.zeros_like(carry_ref)
    acc_ref[...] = jnp.zeros_like(acc_ref)
    @pl.when(qi == 0)
    def _():
        for g in range(hb):
            attend(g, 0, 1, True)

    @pl.when(qi > 0)
    def _():
        key1 = row0 - tk
        for g in range(hb):
            attend(g, key1, 2, True)

        def stick_left():
            return jnp.max(carry_ref[...]) > STICK_GONE

        def body(state):
            m, _ = state
            for g in range(hb):
                attend(g, key1 - (m + 1) * tk, 1, False)
            return m + 1, stick_left()

        lax.while_loop(lambda state: (state[0] < qi - 1) & state[1], body,
                       (jnp.int32(0), stick_left()))

    for g in range(hb):
        o_ref[:, g * HEAD_DIM:(g + 1) * HEAD_DIM] = acc_ref[g].astype(o_ref.dtype)


def _attn_rows(qh, kh, vh, tq, rc, hb):
    h, t, hd = qh.shape
    assert h % hb == 0 and t % tq == 0 and t >= 2 * tq and tq % rc == 0
    resident = pl.BlockSpec((hb, t, hd), lambda hi, qi: (hi, 0, 0), pipeline_mode=pl.Buffered(1))
    return pl.pallas_call(
        functools.partial(_attn_rows_kernel, tq=tq, rc=rc, hb=hb),
        grid=(h // hb, t // tq),
        in_specs=[pl.BlockSpec((hb, tq, hd), lambda hi, qi: (hi, qi, 0)), resident, resident],
        out_specs=pl.BlockSpec((tq, hb * hd), lambda hi, qi: (qi, hi)),
        out_shape=jax.ShapeDtypeStruct((t, h * hd), BF16),
        scratch_shapes=[
            pltpu.VMEM((hb, tq, 2 * tq), F32),
            pltpu.VMEM((hb, tq, 2 * tq), BF16),
            pltpu.VMEM((hb, tq, 2 * tq), F32),
            pltpu.VMEM((hb, 2, tq, 1), F32),
            pltpu.VMEM((hb, tq, 2 * tq), F32),
            pltpu.VMEM((hb, tq, 2 * tq), BF16),
            pltpu.VMEM((hb, tq, hd), F32),
            pltpu.VMEM((hb, tq, 1), F32),
        ],
        compiler_params=_params("parallel", "arbitrary"),
        name="attn_prompt",
    )(qh, kh, vh)


def _attn_sample_kernel(q_ref, kn_ref, vn_ref, ck_ref, cv_ref, o_ref,
                        kt_ref, vt_ref, z_ref, keep_ref, logb_ref, later_ref, w_ref, acc_ref,
                        carry_ref, *, s, tp, tk, pad, rc):
    j = pl.program_id(1)
    n_heads = q_ref.shape[0]
    m = n_heads * s
    chunks = [slice(r0, r0 + rc) for r0 in range(0, m, rc)]
    head_rows = [slice(h * s, (h + 1) * s) for h in range(n_heads)]
    nt = (((1,), (1,)), ((), ()))

    def attend(n, nk, keys_of, values_of, u, masked):
        for h in range(n_heads):
            z_ref[head_rows[h], 0:n] = lax.dot_general(q_ref[h], keys_of(h), nt,
                                                      preferred_element_type=F32)
        for sub in range(n // nk - 1, -1, -1):
            cols = slice(sub * nk, (sub + 1) * nk)
            for rows in chunks:
                valid = None
                if masked:
                    row = lax.broadcasted_iota(jnp.int32, (rc, nk), 0) % s
                    valid = lax.broadcasted_iota(jnp.int32, (rc, nk), 1) < row
                log_keep, log_beta, first = _sb_logs(z_ref[rows, cols], valid)
                keep_ref[rows, 0:nk] = log_keep
                logb_ref[rows, 0:nk] = log_beta
                carry_ref[1, rows, :] = first
            later_ref[:, 0:nk] = jnp.dot(keep_ref[:, 0:nk], u, preferred_element_type=F32)
            for rows in chunks:
                later = later_ref[rows, 0:nk]
                carry = carry_ref[0, rows, :]
                w_ref[rows, cols] = jnp.exp(logb_ref[rows, 0:nk] + later + carry).astype(BF16)
                carry_ref[0, rows, :] = carry + later[:, 0:1] + carry_ref[1, rows, :]
        for h in range(n_heads):
            acc_ref[head_rows[h], :] += jnp.dot(w_ref[head_rows[h], 0:n], values_of(h),
                                                preferred_element_type=F32)

    @pl.when(j == 0)
    def _():
        carry_ref[...] = jnp.zeros_like(carry_ref)
        acc_ref[...] = jnp.zeros_like(acc_ref)
        zeros = jnp.zeros((pad - s, HEAD_DIM), BF16)
        attend(pad, pad,
               lambda h: jnp.concatenate([kn_ref[h], zeros], axis=0),
               lambda h: jnp.concatenate([vn_ref[h], zeros], axis=0),
               _suffix_matrix(pad), True)

    kc = 16
    for c in range(0, tp, kc):
        kt_ref[:, c:c + kc, :] = jnp.swapaxes(ck_ref[c:c + kc], 0, 1).astype(BF16)
        vt_ref[:, c:c + kc, :] = jnp.swapaxes(cv_ref[c:c + kc], 0, 1).astype(BF16)
    attend(tp, tk, lambda h: kt_ref[h], lambda h: vt_ref[h], _suffix_matrix(tk), False)

    @pl.when(j == pl.num_programs(1) - 1)
    def _():
        for h in range(n_heads):
            o_ref[:, h * HEAD_DIM:(h + 1) * HEAD_DIM] = acc_ref[head_rows[h], :].astype(o_ref.dtype)


def _attn_sample(qh, kh, vh, cache_k, cache_v, s, tp, tk, rc):
    h, t, hd = qh.shape
    bn, past = cache_k.shape[:2]
    nkb = past // tp
    m = h * s
    assert rc % s == 0 and m % rc == 0 and tp % tk == 0 and hd <= tk
    new = pl.BlockSpec((h, s, hd), lambda b, j: (0, b, 0))
    old = pl.BlockSpec((None, tp, h, hd), lambda b, j: (b, nkb - 1 - j, 0, 0))
    return pl.pallas_call(
        functools.partial(_attn_sample_kernel, s=s, tp=tp, tk=tk, pad=hd, rc=rc),
        grid=(bn, nkb),
        in_specs=[new, new, new, old, old],
        out_specs=pl.BlockSpec((s, h * hd), lambda b, j: (b, 0)),
        out_shape=jax.ShapeDtypeStruct((t, h * hd), BF16),
        scratch_shapes=[
            pltpu.VMEM((h, tp, hd), BF16),
            pltpu.VMEM((h, tp, hd), BF16),
            pltpu.VMEM((m, tp), F32),
            pltpu.VMEM((m, tk), BF16),
            pltpu.VMEM((m, tk), F32),
            pltpu.VMEM((m, tk), F32),
            pltpu.VMEM((m, tp), BF16),
            pltpu.VMEM((m, hd), F32),
            pltpu.VMEM((2, m, 1), F32),
        ],
        compiler_params=_params("parallel", "arbitrary"),
        name="attn_sample",
    )(qh, kh, vh, cache_k, cache_v)


def _trunk(x, conv_buf, h0, pos0, cache, p, tm, ts):
    bn, s, d = x.shape
    t = bn * s
    x0 = x.reshape(t, d)

    gate, rec = _in_proj(x0, p["a_norm"], p["a_w_in"], tm, 1024)
    hg, h_last, new_buf = _rglru(
        rec.reshape(bn, s, d), gate.reshape(bn, s, d), conv_buf, h0.reshape(bn, 1, d),
        p["a_conv_w"], p["a_conv_b"], p["a_b_r"], p["a_b_i"], p["a_lambda"], p["a_w_r"], p["a_w_i"],
        pos0, ts, 512)
    x1 = _matmul_res(hg.reshape(t, d), p["a_w_out"], x0, tm, 1024)
    x2 = _mlp(x1, p["mlp_norm0"], p["mlp_w_up0"], p["mlp_w_down0"], tm, 1024)

    k, v, kh, vh, qh = _kvq(x2, p["kv_norm"], p["b_norm"], p["w_kv"], p["b_w_q"],
                            p["k_norm"], p["b_q_norm"], tm, 512)
    if cache is None:
        o = _attn_rows(qh, kh, vh, 256, 32, 4)
    else:
        o = _attn_sample(qh, kh, vh, cache[0], cache[1], s, 512, 256, 32)
    x3 = _matmul_res(o, p["b_w_o"], x2, tm, 1024)
    y = _mlp(x3, p["mlp_norm1"], p["mlp_w_up1"], p["mlp_w_down1"], tm, 1024)

    return (y.reshape(bn, s, d), h_last.reshape(1, bn, d), new_buf.reshape(1, bn, CONV_W - 1, d),
            k.reshape(bn, s, N_HEADS, HEAD_DIM), v.reshape(bn, s, N_HEADS, HEAD_DIM))


def kernel(x_prompt, x_sample, state_lru_h, state_conv, cache_k, cache_v, a_norm, a_w_in, a_conv_w, a_conv_b, a_w_r, a_b_r, a_w_i, a_b_i, a_lambda, a_w_out, kv_norm, w_kv, k_norm, b_norm, b_w_q, b_q_norm, b_w_o, mlp_norm, mlp_w_up, mlp_w_down):
    row = lambda a: a.reshape(1, -1)
    p = {
        "a_norm": row(a_norm[0]), "a_w_in": a_w_in[0].astype(BF16),
        "a_conv_w": a_conv_w[0], "a_conv_b": row(a_conv_b[0]),
        "a_w_r": a_w_r[0].astype(BF16), "a_b_r": row(a_b_r[0]),
        "a_w_i": a_w_i[0].astype(BF16), "a_b_i": row(a_b_i[0]),
        "a_lambda": row(a_lambda[0]), "a_w_out": a_w_out[0].astype(BF16),
        "kv_norm": row(kv_norm), "w_kv": w_kv.astype(BF16), "k_norm": row(k_norm),
        "b_norm": row(b_norm[0]), "b_w_q": b_w_q[0].astype(BF16), "b_q_norm": row(b_q_norm[0]),
        "b_w_o": b_w_o[0].astype(BF16),
        "mlp_norm0": row(mlp_norm[0]), "mlp_w_up0": mlp_w_up[0].astype(BF16),
        "mlp_w_down0": mlp_w_down[0].astype(BF16),
        "mlp_norm1": row(mlp_norm[1]), "mlp_w_up1": mlp_w_up[1].astype(BF16),
        "mlp_w_down1": mlp_w_down[1].astype(BF16),
    }
    bp, sp, d = x_prompt.shape
    bs, ss, _ = x_sample.shape
    zero_conv = jnp.zeros((bp, CONV_W - 1, D_RNN), x_prompt.dtype)
    zero_h = jnp.zeros((bp, D_RNN), x_prompt.dtype)
    y_p, p_h, p_conv, p_k, p_v = _trunk(x_prompt, zero_conv, zero_h, 0, None, p, 512, 256)
    cache = (cache_k, cache_v)
    y_s, s_h, s_conv, s_k, s_v = _trunk(x_sample, state_conv[0], state_lru_h[0], PAST_LEN, cache,
                                        p, 512, ss)
    return (y_p, y_s, p_h, p_conv, p_k, p_v, s_h, s_conv, s_k, s_v)
```

```python
import functools

import jax
import jax.numpy as jnp
from jax import lax
from jax.experimental import pallas as pl
from jax.experimental.pallas import tpu as pltpu

F32 = jnp.float32
BF16 = jnp.bfloat16

D_MODEL = 2048
D_RNN = D_MODEL
N_HEADS = 16
HEAD_DIM = D_MODEL // N_HEADS
N_LRU_HEADS = 16
LRU_BLOCK = D_RNN // N_LRU_HEADS
CONV_W = 4
LRU_C = 8.0
D_FF = 4 * D_MODEL
PAST_LEN = 2048
EPS = 1e-6

V7X_VMEM_LIMIT_BYTES = 56 * 1024 * 1024
SUBLANES = 8


def _params(*sem):
    return pltpu.CompilerParams(dimension_semantics=sem, vmem_limit_bytes=V7X_VMEM_LIMIT_BYTES)


def _rmsnorm(x, g):
    y = x * lax.rsqrt(jnp.mean(x * x, axis=-1, keepdims=True) + EPS)
    return y * g


def _softplus(x):
    return jnp.maximum(x, 0.0) + jnp.log(1.0 + jnp.exp(-jnp.abs(x)))


def _head_rmsnorm(t, g):
    outs = []
    for k in range(t.shape[1] // HEAD_DIM):
        outs.append(_rmsnorm(t[:, k * HEAD_DIM:(k + 1) * HEAD_DIM], g))
    return jnp.concatenate(outs, axis=1) if len(outs) > 1 else outs[0]


def _in_proj_kernel(x_ref, g_ref, wg_ref, wr_ref, gate_ref, rec_ref, xn_ref):
    @pl.when(pl.program_id(1) == 0)
    def _():
        xn_ref[...] = _rmsnorm(x_ref[...], g_ref[...]).astype(BF16)

    xn = xn_ref[...]
    gate_ref[...] = jax.nn.gelu(jnp.dot(xn, wg_ref[...], preferred_element_type=F32))
    rec_ref[...] = jnp.dot(xn, wr_ref[...], preferred_element_type=F32)


def _in_proj(x, g, w, tm, tn):
    t, d = x.shape
    nh = D_RNN // tn
    return pl.pallas_call(
        _in_proj_kernel,
        grid=(t // tm, nh),
        in_specs=[
            pl.BlockSpec((tm, d), lambda i, j: (i, 0)),
            pl.BlockSpec((1, d), lambda i, j: (0, 0)),
            pl.BlockSpec((d, tn), lambda i, j: (0, j)),
            pl.BlockSpec((d, tn), lambda i, j: (0, j + nh)),
        ],
        out_specs=[
            pl.BlockSpec((tm, tn), lambda i, j: (i, j)),
            pl.BlockSpec((tm, tn), lambda i, j: (i, j)),
        ],
        out_shape=[jax.ShapeDtypeStruct((t, D_RNN), F32)] * 2,
        scratch_shapes=[pltpu.VMEM((tm, d), BF16)],
        compiler_params=_params("parallel", "arbitrary"),
        name="in_proj",
    )(x, g, w, w)


def _rglru_kernel(rec_ref, gate_ref, cbuf_ref, h0_ref, cw_ref, cb_ref, br_ref, bi_ref, lam_ref,
                  wr_ref, wi_ref, hg_ref, hlast_ref, nbuf_ref, ext_ref, hc_ref, *, ts, tc, pos0):
    t = pl.program_id(2)
    pad = SUBLANES

    @pl.when(t == 0)
    def _():
        ext_ref[pad - (CONV_W - 1):pad, :] = cbuf_ref[...]
        hc_ref[...] = h0_ref[...]

    ext_ref[pad:pad + ts, :] = rec_ref[...]
    c = cb_ref[...] + ext_ref[pad - 3:pad - 3 + ts, :] * cw_ref[0:1, :]
    for j in range(1, CONV_W):
        c = c + ext_ref[pad - 3 + j:pad - 3 + j + ts, :] * cw_ref[j:j + 1, :]

    c16 = c.astype(BF16)
    rs, is_ = [], []
    for k in range(tc // LRU_BLOCK):
        blk = c16[:, k * LRU_BLOCK:(k + 1) * LRU_BLOCK]
        rs.append(jnp.dot(blk, wr_ref[k], preferred_element_type=F32))
        is_.append(jnp.dot(blk, wi_ref[k], preferred_element_type=F32))
    r = jax.nn.sigmoid(jnp.concatenate(rs, axis=1) + br_ref[...])
    ig = jax.nn.sigmoid(jnp.concatenate(is_, axis=1) + bi_ref[...])

    log_a = (-LRU_C * r) * _softplus(-lam_ref[...])
    a = jnp.exp(log_a)
    m2 = jnp.tanh(-log_a) * (a * a + 1.0)
    mult = m2 * lax.rsqrt(jnp.maximum(m2, jnp.finfo(F32).tiny))
    row = lax.broadcasted_iota(jnp.int32, (ts, tc), 0)
    if pos0 == 0:
        mult = jnp.where((row == 0) & (t == 0), 1.0, mult)
    b = mult * ig * c

    groups = ts // SUBLANES
    a = a.reshape(groups, SUBLANES, tc)
    b = b.reshape(groups, SUBLANES, tc)
    sub = lax.broadcasted_iota(jnp.int32, (groups, SUBLANES, tc), 1)
    d = 1
    while d < SUBLANES:
        keep = sub >= d
        a_sh = jnp.where(keep, pltpu.roll(a, d, 1), 1.0)
        b_sh = jnp.where(keep, pltpu.roll(b, d, 1), 0.0)
        b = a * b_sh + b
        a = a * a_sh
        d *= 2
    state = hc_ref[...]
    for g in range(0, groups, 2):
        h0 = b[g] + a[g] * state
        h1 = b[g + 1] + a[g + 1] * h0[SUBLANES - 1:SUBLANES, :]
        state = h1[SUBLANES - 1:SUBLANES, :]
        rows = slice(g * SUBLANES, (g + 2) * SUBLANES)
        h = jnp.concatenate([h0, h1], axis=0)
        hg_ref[rows, :] = (h * gate_ref[rows, :]).astype(hg_ref.dtype)

    hc_ref[...] = state
    hlast_ref[...] = state
    nbuf_ref[...] = ext_ref[pad + ts - (CONV_W - 1):pad + ts, :]
    ext_ref[0:pad, :] = ext_ref[ts:ts + pad, :]


def _rglru(rec, gate, cbuf, h0, cw, cb, br, bi, lam, wr, wi, pos0, ts, tc):
    bn, s, c = rec.shape
    nblk = tc // LRU_BLOCK
    seq = pl.BlockSpec((None, ts, tc), lambda b, ci, ti: (b, ti, ci))
    vec = pl.BlockSpec((1, tc), lambda b, ci, ti: (0, ci))
    return pl.pallas_call(
        functools.partial(_rglru_kernel, ts=ts, tc=tc, pos0=pos0),
        grid=(bn, c // tc, s // ts),
        in_specs=[
            seq, seq,
            pl.BlockSpec((None, CONV_W - 1, tc), lambda b, ci, ti: (b, 0, ci)),
            pl.BlockSpec((None, 1, tc), lambda b, ci, ti: (b, 0, ci)),
            pl.BlockSpec((CONV_W, tc), lambda b, ci, ti: (0, ci)),
            vec, vec, vec, vec,
            pl.BlockSpec((nblk, LRU_BLOCK, LRU_BLOCK), lambda b, ci, ti: (ci, 0, 0)),
            pl.BlockSpec((nblk, LRU_BLOCK, LRU_BLOCK), lambda b, ci, ti: (ci, 0, 0)),
        ],
        out_specs=[
            seq,
            pl.BlockSpec((None, 1, tc), lambda b, ci, ti: (b, 0, ci)),
            pl.BlockSpec((None, CONV_W - 1, tc), lambda b, ci, ti: (b, 0, ci)),
        ],
        out_shape=[
            jax.ShapeDtypeStruct((bn, s, c), BF16),
            jax.ShapeDtypeStruct((bn, 1, c), F32),
            jax.ShapeDtypeStruct((bn, CONV_W - 1, c), F32),
        ],
        scratch_shapes=[pltpu.VMEM((ts + 2 * SUBLANES, tc), F32), pltpu.VMEM((1, tc), F32)],
        compiler_params=_params("parallel", "parallel", "arbitrary"),
        name="rglru",
    )(rec, gate, cbuf, h0, cw, cb, br, bi, lam, wr, wi)


def _matmul_res_kernel(a_ref, w_ref, res_ref, o_ref):
    o_ref[...] = res_ref[...] + jnp.dot(a_ref[...], w_ref[...], preferred_element_type=F32)


def _matmul_res(a, w, res, tm, tn):
    t, k = a.shape
    n = w.shape[1]
    return pl.pallas_call(
        _matmul_res_kernel,
        grid=(t // tm, n // tn),
        in_specs=[
            pl.BlockSpec((tm, k), lambda i, j: (i, 0)),
            pl.BlockSpec((k, tn), lambda i, j: (0, j)),
            pl.BlockSpec((tm, tn), lambda i, j: (i, j)),
        ],
        out_specs=pl.BlockSpec((tm, tn), lambda i, j: (i, j)),
        out_shape=jax.ShapeDtypeStruct((t, n), F32),
        compiler_params=_params("parallel", "arbitrary"),
        name="matmul_res",
    )(a, w, res)


def _mlp_kernel(x_ref, g_ref, wu_ref, wd_ref, o_ref, xn_ref):
    @pl.when(pl.program_id(1) == 0)
    def _():
        x = x_ref[...]
        xn_ref[...] = _rmsnorm(x, g_ref[...]).astype(BF16)
        o_ref[...] = x

    h = jnp.maximum(jnp.dot(xn_ref[...], wu_ref[...], preferred_element_type=F32), 0.0)
    o_ref[...] += jnp.dot((h * h).astype(BF16), wd_ref[...], preferred_element_type=F32)


def _mlp(x, g, wu, wd, tm, tf):
    t, d = x.shape
    f = wu.shape[1]
    return pl.pallas_call(
        _mlp_kernel,
        grid=(t // tm, f // tf),
        in_specs=[
            pl.BlockSpec((tm, d), lambda i, j: (i, 0)),
            pl.BlockSpec((1, d), lambda i, j: (0, 0)),
            pl.BlockSpec((d, tf), lambda i, j: (0, j)),
            pl.BlockSpec((tf, d), lambda i, j: (j, 0)),
        ],
        out_specs=pl.BlockSpec((tm, d), lambda i, j: (i, 0)),
        out_shape=jax.ShapeDtypeStruct((t, d), F32),
        scratch_shapes=[pltpu.VMEM((tm, d), BF16)],
        compiler_params=_params("parallel", "arbitrary"),
        name="mlp",
    )(x, g, wu, wd)


def _kvq_kernel(x_ref, gkv_ref, gq_ref, wk_ref, wv_ref, wq_ref, kn_ref, qn_ref,
                k_ref, v_ref, kh_ref, vh_ref, qh_ref, xkv_ref, xq_ref):
    @pl.when(pl.program_id(1) == 0)
    def _():
        x = x_ref[...]
        xhat = x * lax.rsqrt(jnp.mean(x * x, axis=-1, keepdims=True) + EPS)
        xkv_ref[...] = (xhat * gkv_ref[...]).astype(BF16)
        xq_ref[...] = (xhat * gq_ref[...]).astype(BF16)

    xkv = xkv_ref[...]
    k = _head_rmsnorm(jnp.dot(xkv, wk_ref[...], preferred_element_type=F32), kn_ref[...])
    v = jnp.dot(xkv, wv_ref[...], preferred_element_type=F32)
    q = _head_rmsnorm(jnp.dot(xq_ref[...], wq_ref[...], preferred_element_type=F32), qn_ref[...])
    k_ref[...] = k
    v_ref[...] = v
    for hh in range(kh_ref.shape[0]):
        sl = slice(hh * HEAD_DIM, (hh + 1) * HEAD_DIM)
        kh_ref[hh] = k[:, sl].astype(BF16)
        vh_ref[hh] = v[:, sl].astype(BF16)
        qh_ref[hh] = q[:, sl].astype(BF16)


def _kvq(x, gkv, gq, wkv, wq, kn, qn, tm, tn):
    t, d = x.shape
    nh = D_MODEL // tn
    hb = tn // HEAD_DIM
    tile = pl.BlockSpec((tm, tn), lambda i, j: (i, j))
    heads = pl.BlockSpec((hb, tm, HEAD_DIM), lambda i, j: (j, i, 0))
    vec = pl.BlockSpec((1, d), lambda i, j: (0, 0))
    hvec = pl.BlockSpec((1, HEAD_DIM), lambda i, j: (0, 0))
    return pl.pallas_call(
        _kvq_kernel,
        grid=(t // tm, nh),
        in_specs=[
            pl.BlockSpec((tm, d), lambda i, j: (i, 0)),
            vec, vec,
            pl.BlockSpec((d, tn), lambda i, j: (0, j)),
            pl.BlockSpec((d, tn), lambda i, j: (0, j + nh)),
            pl.BlockSpec((d, tn), lambda i, j: (0, j)),
            hvec, hvec,
        ],
        out_specs=[tile, tile, heads, heads, heads],
        out_shape=[
            jax.ShapeDtypeStruct((t, D_MODEL), F32),
            jax.ShapeDtypeStruct((t, D_MODEL), F32),
            jax.ShapeDtypeStruct((N_HEADS, t, HEAD_DIM), BF16),
            jax.ShapeDtypeStruct((N_HEADS, t, HEAD_DIM), BF16),
            jax.ShapeDtypeStruct((N_HEADS, t, HEAD_DIM), BF16),
        ],
        scratch_shapes=[pltpu.VMEM((tm, d), BF16), pltpu.VMEM((tm, d), BF16)],
        compiler_params=_params("parallel", "arbitrary"),
        name="kvq",
    )(x, gkv, gq, wkv, wkv, wq, kn, qn)


def _suffix_matrix(n):
    r = lax.broadcasted_iota(jnp.int32, (n, n), 0)
    c = lax.broadcasted_iota(jnp.int32, (n, n), 1)
    return jnp.where(r > c, 1.0, 0.0).astype(BF16)


NEG_BIG = -1e30
STICK_GONE = -110.0
LOG2E = 1.4426950408889634


def _sb_logs(zr, valid):
    scale = HEAD_DIM ** -0.5
    nz = zr * (-scale)
    tail = jnp.log(1.0 + jnp.exp2(jnp.abs(zr) * (-scale * LOG2E)))
    log_keep = jnp.minimum(nz, 0.0) - tail
    log_beta = log_keep - nz
    if valid is not None:
        log_keep = jnp.where(valid, log_keep, 0.0)
        log_beta = jnp.where(valid, log_beta, NEG_BIG)
    return log_keep.astype(BF16), log_beta, log_keep[:, 0:1]


def _attn_rows_kernel(q_ref, k_ref, v_ref, o_ref, z_ref, keep_ref, logb_ref, first_ref, later_ref,
                      w_ref, acc_ref, carry_ref, *, tq, rc, hb):
    qi = pl.program_id(1)
    tk = tq
    row0 = qi * tq
    u = _suffix_matrix(tk)
    chunks = [slice(r0, r0 + rc) for r0 in range(0, tq, rc)]
    nt = (((1,), (1,)), ((), ()))

    def attend(g, key0, nblk, diagonal):
        n = nblk * tk
        ks = pl.ds(pl.multiple_of(key0, tk), n)
        z_ref[g, :, 0:n] = lax.dot_general(q_ref[g], k_ref[g, ks, :], nt, preferred_element_type=F32)
        for blk in range(nblk - 1, -1, -1):
            cols = slice(blk * tk, (blk + 1) * tk)
            for rows in chunks:
                valid = None
                if diagonal and blk == nblk - 1:
                    row = lax.broadcasted_iota(jnp.int32, (rc, tk), 0) + rows.start
                    valid = lax.broadcasted_iota(jnp.int32, (rc, tk), 1) < row
                log_keep, log_beta, first = _sb_logs(z_ref[g, rows, cols], valid)
                keep_ref[g, rows, cols] = log_keep
                logb_ref[g, rows, cols] = log_beta
                first_ref[g, blk, rows, :] = first
            later_ref[g, :, cols] = jnp.dot(keep_ref[g, :, cols], u, preferred_element_type=F32)
            for rows in chunks:
                later = later_ref[g, rows, cols]
                carry = carry_ref[g, rows, :]
                w_ref[g, rows, cols] = jnp.exp(logb_ref[g, rows, cols] + later + carry).astype(BF16)
                carry_ref[g, rows, :] = carry + later[:, 0:1] + first_ref[g, blk, rows, :]
        acc_ref[g] += jnp.dot(w_ref[g, :, 0:n], v_ref[g, ks, :], preferred_element_type=F32)

    carry_ref[...] = jnp.zeros_like(carry_ref)
    acc_ref[...] = jnp.zeros_like(acc_ref)
    @pl.when(qi == 0)
    def _():
        for g in range(hb):
            attend(g, 0, 1, True)

    @pl.when(qi > 0)
    def _():
        key1 = row0 - tk
        for g in range(hb):
            attend(g, key1, 2, True)

        def stick_left():
            return jnp.max(carry_ref[...]) > STICK_GONE

        def body(state):
            m, _ = state
            for g in range(hb):
                attend(g, key1 - (m + 1) * tk, 1, False)
            return m + 1, stick_left()

        lax.while_loop(lambda state: (state[0] < qi - 1) & state[1], body,
                       (jnp.int32(0), stick_left()))

    for g in range(hb):
        o_ref[:, g * HEAD_DIM:(g + 1) * HEAD_DIM] = acc_ref[g].astype(o_ref.dtype)


def _attn_rows(qh, kh, vh, tq, rc, hb):
    h, t, hd = qh.shape
    assert h % hb == 0 and t % tq == 0 and t >= 2 * tq and tq % rc == 0
    resident = pl.BlockSpec((hb, t, hd), lambda hi, qi: (hi, 0, 0), pipeline_mode=pl.Buffered(1))
    return pl.pallas_call(
        functools.partial(_attn_rows_kernel, tq=tq, rc=rc, hb=hb),
        grid=(h // hb, t // tq),
        in_specs=[pl.BlockSpec((hb, tq, hd), lambda hi, qi: (hi, qi, 0)), resident, resident],
        out_specs=pl.BlockSpec((tq, hb * hd), lambda hi, qi: (qi, hi)),
        out_shape=jax.ShapeDtypeStruct((t, h * hd), BF16),
        scratch_shapes=[
            pltpu.VMEM((hb, tq, 2 * tq), F32),
            pltpu.VMEM((hb, tq, 2 * tq), BF16),
            pltpu.VMEM((hb, tq, 2 * tq), F32),
            pltpu.VMEM((hb, 2, tq, 1), F32),
            pltpu.VMEM((hb, tq, 2 * tq), F32),
            pltpu.VMEM((hb, tq, 2 * tq), BF16),
            pltpu.VMEM((hb, tq, hd), F32),
            pltpu.VMEM((hb, tq, 1), F32),
        ],
        compiler_params=_params("parallel", "arbitrary"),
        name="attn_prompt",
    )(qh, kh, vh)


def _attn_sample_kernel(q_ref, kn_ref, vn_ref, ck_hbm, cv_hbm, o_ref,
                        kbuf_ref, vbuf_ref, sem, kt_ref, vt_ref, z_ref, keep_ref, logb_ref,
                        later_ref, w_ref, acc_ref, carry_ref, *, s, tp, tk, pad, rc):
    b = pl.program_id(0)
    n_heads = q_ref.shape[0]
    nkb = ck_hbm.shape[1] // tp
    m = n_heads * s
    chunks = [slice(r0, r0 + rc) for r0 in range(0, m, rc)]
    head_rows = [slice(h * s, (h + 1) * s) for h in range(n_heads)]
    nt = (((1,), (1,)), ((), ()))
    slot = b % 2

    def cache_copies(row, blk, into):
        keys = pl.ds(pl.multiple_of((nkb - 1 - blk) * tp, tp), tp)
        return (pltpu.make_async_copy(ck_hbm.at[row, keys], kbuf_ref.at[into], sem.at[into, 0]),
                pltpu.make_async_copy(cv_hbm.at[row, keys], vbuf_ref.at[into], sem.at[into, 1]))

    def attend(n, nk, keys_of, values_of, u, masked):
        for h in range(n_heads):
            z_ref[head_rows[h], 0:n] = lax.dot_general(q_ref[h], keys_of(h), nt,
                                                      preferred_element_type=F32)
        for sub in range(n // nk - 1, -1, -1):
            cols = slice(sub * nk, (sub + 1) * nk)
            for rows in chunks:
                valid = None
                if masked:
                    row = lax.broadcasted_iota(jnp.int32, (rc, nk), 0) % s
                    valid = lax.broadcasted_iota(jnp.int32, (rc, nk), 1) < row
                log_keep, log_beta, first = _sb_logs(z_ref[rows, cols], valid)
                keep_ref[rows, 0:nk] = log_keep
                logb_ref[rows, 0:nk] = log_beta
                carry_ref[1, rows, :] = first
            later_ref[:, 0:nk] = jnp.dot(keep_ref[:, 0:nk], u, preferred_element_type=F32)
            for rows in chunks:
                later = later_ref[rows, 0:nk]
                carry = carry_ref[0, rows, :]
                w_ref[rows, cols] = jnp.exp(logb_ref[rows, 0:nk] + later + carry).astype(BF16)
                carry_ref[0, rows, :] = carry + later[:, 0:1] + carry_ref[1, rows, :]
        for h in range(n_heads):
            acc_ref[head_rows[h], :] += jnp.dot(w_ref[head_rows[h], 0:n], values_of(h),
                                                preferred_element_type=F32)

    def attend_cache_block():
        kc = 16
        for c in range(0, tp, kc):
            kt_ref[:, c:c + kc, :] = jnp.swapaxes(kbuf_ref[slot, c:c + kc], 0, 1).astype(BF16)
            vt_ref[:, c:c + kc, :] = jnp.swapaxes(vbuf_ref[slot, c:c + kc], 0, 1).astype(BF16)
        attend(tp, tk, lambda h: kt_ref[h], lambda h: vt_ref[h], _suffix_matrix(tk), False)

    @pl.when(b == 0)
    def _():
        for copy in cache_copies(0, 0, 0):
            copy.start()

    @pl.when(b + 1 < pl.num_programs(0))
    def _():
        for copy in cache_copies(b + 1, 0, 1 - slot):
            copy.start()

    carry_ref[...] = jnp.zeros_like(carry_ref)
    acc_ref[...] = jnp.zeros_like(acc_ref)
    zeros = jnp.zeros((pad - s, HEAD_DIM), BF16)
    attend(pad, pad,
           lambda h: jnp.concatenate([kn_ref[h], zeros], axis=0),
           lambda h: jnp.concatenate([vn_ref[h], zeros], axis=0),
           _suffix_matrix(pad), True)

    for copy in cache_copies(b, 0, slot):
        copy.wait()
    attend_cache_block()

    def stick_left():
        return jnp.max(carry_ref[0]) > STICK_GONE

    def older(state):
        blk, _ = state
        copies = cache_copies(b, blk, slot)
        for copy in copies:
            copy.start()
        for copy in copies:
            copy.wait()
        attend_cache_block()
        return blk + 1, stick_left()

    lax.while_loop(lambda state: (state[0] < nkb) & state[1], older, (jnp.int32(1), stick_left()))

    for h in range(n_heads):
        o_ref[:, h * HEAD_DIM:(h + 1) * HEAD_DIM] = acc_ref[head_rows[h], :].astype(o_ref.dtype)


def _attn_sample(qh, kh, vh, cache_k, cache_v, s, tp, tk, rc):
    h, t, hd = qh.shape
    bn, past = cache_k.shape[:2]
    m = h * s
    assert rc % s == 0 and m % rc == 0 and past % tp == 0 and tp % tk == 0 and hd <= tk
    new = pl.BlockSpec((h, s, hd), lambda b: (0, b, 0))
    hbm = pl.BlockSpec(memory_space=pl.ANY)
    return pl.pallas_call(
        functools.partial(_attn_sample_kernel, s=s, tp=tp, tk=tk, pad=hd, rc=rc),
        grid=(bn,),
        in_specs=[new, new, new, hbm, hbm],
        out_specs=pl.BlockSpec((s, h * hd), lambda b: (b, 0)),
        out_shape=jax.ShapeDtypeStruct((t, h * hd), BF16),
        scratch_shapes=[
            pltpu.VMEM((2, tp, h, hd), F32),
            pltpu.VMEM((2, tp, h, hd), F32),
            pltpu.SemaphoreType.DMA((2, 2)),
            pltpu.VMEM((h, tp, hd), BF16),
            pltpu.VMEM((h, tp, hd), BF16),
            pltpu.VMEM((m, tp), F32),
            pltpu.VMEM((m, tk), BF16),
            pltpu.VMEM((m, tk), F32),
            pltpu.VMEM((m, tk), F32),
            pltpu.VMEM((m, tp), BF16),
            pltpu.VMEM((m, hd), F32),
            pltpu.VMEM((2, m, 1), F32),
        ],
        compiler_params=_params("arbitrary"),
        name="attn_sample",
    )(qh, kh, vh, cache_k, cache_v)


def _trunk(x, conv_buf, h0, pos0, cache, p, tm, ts):
    bn, s, d = x.shape
    t = bn * s
    x0 = x.reshape(t, d)
    tm_wide = min(2 * tm, t)

    gate, rec = _in_proj(x0, p["a_norm"], p["a_w_in"], tm_wide, 512)
    hg, h_last, new_buf = _rglru(
        rec.reshape(bn, s, d), gate.reshape(bn, s, d), conv_buf, h0.reshape(bn, 1, d),
        p["a_conv_w"], p["a_conv_b"], p["a_b_r"], p["a_b_i"], p["a_lambda"], p["a_w_r"], p["a_w_i"],
        pos0, ts, 512)
    x1 = _matmul_res(hg.reshape(t, d), p["a_w_out"], x0, tm, d)
    x2 = _mlp(x1, p["mlp_norm0"], p["mlp_w_up0"], p["mlp_w_down0"], tm_wide, 512)

    k, v, kh, vh, qh = _kvq(x2, p["kv_norm"], p["b_norm"], p["w_kv"], p["b_w_q"],
                            p["k_norm"], p["b_q_norm"], tm, 512)
    if cache is None:
        o = _attn_rows(qh, kh, vh, 256, 32, 4)
    else:
        o = _attn_sample(qh, kh, vh, cache[0], cache[1], s, 512, 256, 32)
    x3 = _matmul_res(o, p["b_w_o"], x2, tm, d)
    y = _mlp(x3, p["mlp_norm1"], p["mlp_w_up1"], p["mlp_w_down1"], tm_wide, 512)

    return (y.reshape(bn, s, d), h_last.reshape(1, bn, d), new_buf.reshape(1, bn, CONV_W - 1, d),
            k.reshape(bn, s, N_HEADS, HEAD_DIM), v.reshape(bn, s, N_HEADS, HEAD_DIM))


def kernel(x_prompt, x_sample, state_lru_h, state_conv, cache_k, cache_v, a_norm, a_w_in, a_conv_w, a_conv_b, a_w_r, a_b_r, a_w_i, a_b_i, a_lambda, a_w_out, kv_norm, w_kv, k_norm, b_norm, b_w_q, b_q_norm, b_w_o, mlp_norm, mlp_w_up, mlp_w_down):
    row = lambda a: a.reshape(1, -1)
    p = {
        "a_norm": row(a_norm[0]), "a_w_in": a_w_in[0].astype(BF16),
        "a_conv_w": a_conv_w[0], "a_conv_b": row(a_conv_b[0]),
        "a_w_r": a_w_r[0].astype(BF16), "a_b_r": row(a_b_r[0]),
        "a_w_i": a_w_i[0].astype(BF16), "a_b_i": row(a_b_i[0]),
        "a_lambda": row(a_lambda[0]), "a_w_out": a_w_out[0].astype(BF16),
        "kv_norm": row(kv_norm), "w_kv": w_kv.astype(BF16), "k_norm": row(k_norm),
        "b_norm": row(b_norm[0]), "b_w_q": b_w_q[0].astype(BF16), "b_q_norm": row(b_q_norm[0]),
        "b_w_o": b_w_o[0].astype(BF16),
        "mlp_norm0": row(mlp_norm[0]), "mlp_w_up0": mlp_w_up[0].astype(BF16),
        "mlp_w_down0": mlp_w_down[0].astype(BF16),
        "mlp_norm1": row(mlp_norm[1]), "mlp_w_up1": mlp_w_up[1].astype(BF16),
        "mlp_w_down1": mlp_w_down[1].astype(BF16),
    }
    bp, sp, d = x_prompt.shape
    bs, ss, _ = x_sample.shape
    zero_conv = jnp.zeros((bp, CONV_W - 1, D_RNN), x_prompt.dtype)
    zero_h = jnp.zeros((bp, D_RNN), x_prompt.dtype)
    y_p, p_h, p_conv, p_k, p_v = _trunk(x_prompt, zero_conv, zero_h, 0, None, p, 512, 256)
    cache = (cache_k, cache_v)
    y_s, s_h, s_conv, s_k, s_v = _trunk(x_sample, state_conv[0], state_lru_h[0], PAST_LEN, cache,
                                        p, 512, ss)
    return (y_p, y_s, p_h, p_conv, p_k, p_v, s_h, s_conv, s_k, s_v)
```

```python
import functools

import jax
import jax.numpy as jnp
from jax import lax
from jax.experimental import pallas as pl
from jax.experimental.pallas import tpu as pltpu

F32 = jnp.float32
BF16 = jnp.bfloat16

D_MODEL = 2048
D_RNN = D_MODEL
N_HEADS = 16
HEAD_DIM = D_MODEL // N_HEADS
N_LRU_HEADS = 16
LRU_BLOCK = D_RNN // N_LRU_HEADS
CONV_W = 4
LRU_C = 8.0
D_FF = 4 * D_MODEL
PAST_LEN = 2048
EPS = 1e-6

V7X_VMEM_LIMIT_BYTES = 56 * 1024 * 1024
SUBLANES = 8


def _params(*sem):
    return pltpu.CompilerParams(dimension_semantics=sem, vmem_limit_bytes=V7X_VMEM_LIMIT_BYTES)


def _rmsnorm(x, g):
    y = x * lax.rsqrt(jnp.mean(x * x, axis=-1, keepdims=True) + EPS)
    return y * g


def _softplus(x):
    return jnp.maximum(x, 0.0) + jnp.log(1.0 + jnp.exp(-jnp.abs(x)))


def _head_rmsnorm(t, g):
    outs = []
    for k in range(t.shape[1] // HEAD_DIM):
        outs.append(_rmsnorm(t[:, k * HEAD_DIM:(k + 1) * HEAD_DIM], g))
    return jnp.concatenate(outs, axis=1) if len(outs) > 1 else outs[0]


def _in_proj_kernel(x_ref, g_ref, wg_ref, wr_ref, gate_ref, rec_ref, xn_ref):
    @pl.when(pl.program_id(1) == 0)
    def _():
        xn_ref[...] = _rmsnorm(x_ref[...], g_ref[...]).astype(BF16)

    xn = xn_ref[...]
    gate_ref[...] = jax.nn.gelu(jnp.dot(xn, wg_ref[...], preferred_element_type=F32))
    rec_ref[...] = jnp.dot(xn, wr_ref[...], preferred_element_type=F32)


def _in_proj(x, g, w, tm, tn):
    t, d = x.shape
    nh = D_RNN // tn
    return pl.pallas_call(
        _in_proj_kernel,
        grid=(t // tm, nh),
        in_specs=[
            pl.BlockSpec((tm, d), lambda i, j: (i, 0)),
            pl.BlockSpec((1, d), lambda i, j: (0, 0)),
            pl.BlockSpec((d, tn), lambda i, j: (0, j)),
            pl.BlockSpec((d, tn), lambda i, j: (0, j + nh)),
        ],
        out_specs=[
            pl.BlockSpec((tm, tn), lambda i, j: (i, j)),
            pl.BlockSpec((tm, tn), lambda i, j: (i, j)),
        ],
        out_shape=[jax.ShapeDtypeStruct((t, D_RNN), F32)] * 2,
        scratch_shapes=[pltpu.VMEM((tm, d), BF16)],
        compiler_params=_params("parallel", "arbitrary"),
        name="in_proj",
    )(x, g, w, w)


def _rglru_kernel(rec_ref, gate_ref, cbuf_ref, h0_ref, cw_ref, cb_ref, br_ref, bi_ref, lam_ref,
                  wr_ref, wi_ref, hg_ref, hlast_ref, nbuf_ref, ext_ref, hc_ref, *, ts, tc, pos0):
    t = pl.program_id(2)
    pad = SUBLANES

    @pl.when(t == 0)
    def _():
        ext_ref[pad - (CONV_W - 1):pad, :] = cbuf_ref[...]
        hc_ref[...] = h0_ref[...]

    ext_ref[pad:pad + ts, :] = rec_ref[...]
    c = cb_ref[...] + ext_ref[pad - 3:pad - 3 + ts, :] * cw_ref[0:1, :]
    for j in range(1, CONV_W):
        c = c + ext_ref[pad - 3 + j:pad - 3 + j + ts, :] * cw_ref[j:j + 1, :]

    c16 = c.astype(BF16)
    rs, is_ = [], []
    for k in range(tc // LRU_BLOCK):
        blk = c16[:, k * LRU_BLOCK:(k + 1) * LRU_BLOCK]
        rs.append(jnp.dot(blk, wr_ref[k], preferred_element_type=F32))
        is_.append(jnp.dot(blk, wi_ref[k], preferred_element_type=F32))
    r = jax.nn.sigmoid(jnp.concatenate(rs, axis=1) + br_ref[...])
    ig = jax.nn.sigmoid(jnp.concatenate(is_, axis=1) + bi_ref[...])

    log_a = (-LRU_C * r) * _softplus(-lam_ref[...])
    a = jnp.exp(log_a)
    m2 = jnp.tanh(-log_a) * (a * a + 1.0)
    mult = m2 * lax.rsqrt(jnp.maximum(m2, jnp.finfo(F32).tiny))
    row = lax.broadcasted_iota(jnp.int32, (ts, tc), 0)
    if pos0 == 0:
        mult = jnp.where((row == 0) & (t == 0), 1.0, mult)
    b = mult * ig * c

    groups = ts // SUBLANES
    a = a.reshape(groups, SUBLANES, tc)
    b = b.reshape(groups, SUBLANES, tc)
    sub = lax.broadcasted_iota(jnp.int32, (groups, SUBLANES, tc), 1)
    d = 1
    while d < SUBLANES:
        keep = sub >= d
        a_sh = jnp.where(keep, pltpu.roll(a, d, 1), 1.0)
        b_sh = jnp.where(keep, pltpu.roll(b, d, 1), 0.0)
        b = a * b_sh + b
        a = a * a_sh
        d *= 2
    state = hc_ref[...]
    for g in range(0, groups, 2):
        h0 = b[g] + a[g] * state
        h1 = b[g + 1] + a[g + 1] * h0[SUBLANES - 1:SUBLANES, :]
        state = h1[SUBLANES - 1:SUBLANES, :]
        rows = slice(g * SUBLANES, (g + 2) * SUBLANES)
        h = jnp.concatenate([h0, h1], axis=0)
        hg_ref[rows, :] = (h * gate_ref[rows, :]).astype(hg_ref.dtype)

    hc_ref[...] = state
    hlast_ref[...] = state
    nbuf_ref[...] = ext_ref[pad + ts - (CONV_W - 1):pad + ts, :]
    ext_ref[0:pad, :] = ext_ref[ts:ts + pad, :]


def _rglru(rec, gate, cbuf, h0, cw, cb, br, bi, lam, wr, wi, pos0, ts, tc):
    bn, s, c = rec.shape
    nblk = tc // LRU_BLOCK
    seq = pl.BlockSpec((None, ts, tc), lambda b, ci, ti: (b, ti, ci))
    vec = pl.BlockSpec((1, tc), lambda b, ci, ti: (0, ci))
    return pl.pallas_call(
        functools.partial(_rglru_kernel, ts=ts, tc=tc, pos0=pos0),
        grid=(bn, c // tc, s // ts),
        in_specs=[
            seq, seq,
            pl.BlockSpec((None, CONV_W - 1, tc), lambda b, ci, ti: (b, 0, ci)),
            pl.BlockSpec((None, 1, tc), lambda b, ci, ti: (b, 0, ci)),
            pl.BlockSpec((CONV_W, tc), lambda b, ci, ti: (0, ci)),
            vec, vec, vec, vec,
            pl.BlockSpec((nblk, LRU_BLOCK, LRU_BLOCK), lambda b, ci, ti: (ci, 0, 0)),
            pl.BlockSpec((nblk, LRU_BLOCK, LRU_BLOCK), lambda b, ci, ti: (ci, 0, 0)),
        ],
        out_specs=[
            seq,
            pl.BlockSpec((None, 1, tc), lambda b, ci, ti: (b, 0, ci)),
            pl.BlockSpec((None, CONV_W - 1, tc), lambda b, ci, ti: (b, 0, ci)),
        ],
        out_shape=[
            jax.ShapeDtypeStruct((bn, s, c), BF16),
            jax.ShapeDtypeStruct((bn, 1, c), F32),
            jax.ShapeDtypeStruct((bn, CONV_W - 1, c), F32),
        ],
        scratch_shapes=[pltpu.VMEM((ts + 2 * SUBLANES, tc), F32), pltpu.VMEM((1, tc), F32)],
        compiler_params=_params("parallel", "parallel", "arbitrary"),
        name="rglru",
    )(rec, gate, cbuf, h0, cw, cb, br, bi, lam, wr, wi)


def _matmul_res_kernel(a_ref, w_ref, res_ref, o_ref):
    o_ref[...] = res_ref[...] + jnp.dot(a_ref[...], w_ref[...], preferred_element_type=F32)


def _matmul_res(a, w, res, tm, tn):
    t, k = a.shape
    n = w.shape[1]
    return pl.pallas_call(
        _matmul_res_kernel,
        grid=(t // tm, n // tn),
        in_specs=[
            pl.BlockSpec((tm, k), lambda i, j: (i, 0)),
            pl.BlockSpec((k, tn), lambda i, j: (0, j)),
            pl.BlockSpec((tm, tn), lambda i, j: (i, j)),
        ],
        out_specs=pl.BlockSpec((tm, tn), lambda i, j: (i, j)),
        out_shape=jax.ShapeDtypeStruct((t, n), F32),
        compiler_params=_params("parallel", "arbitrary"),
        name="matmul_res",
    )(a, w, res)


def _mlp_kernel(x_ref, g_ref, wu_ref, wd_ref, o_ref, xn_ref):
    @pl.when(pl.program_id(1) == 0)
    def _():
        x = x_ref[...]
        xn_ref[...] = _rmsnorm(x, g_ref[...]).astype(BF16)
        o_ref[...] = x

    h = jnp.maximum(jnp.dot(xn_ref[...], wu_ref[...], preferred_element_type=F32), 0.0)
    o_ref[...] += jnp.dot((h * h).astype(BF16), wd_ref[...], preferred_element_type=F32)


def _mlp(x, g, wu, wd, tm, tf):
    t, d = x.shape
    f = wu.shape[1]
    return pl.pallas_call(
        _mlp_kernel,
        grid=(t // tm, f // tf),
        in_specs=[
            pl.BlockSpec((tm, d), lambda i, j: (i, 0)),
            pl.BlockSpec((1, d), lambda i, j: (0, 0)),
            pl.BlockSpec((d, tf), lambda i, j: (0, j)),
            pl.BlockSpec((tf, d), lambda i, j: (j, 0)),
        ],
        out_specs=pl.BlockSpec((tm, d), lambda i, j: (i, 0)),
        out_shape=jax.ShapeDtypeStruct((t, d), F32),
        scratch_shapes=[pltpu.VMEM((tm, d), BF16)],
        compiler_params=_params("parallel", "arbitrary"),
        name="mlp",
    )(x, g, wu, wd)


def _kvq_kernel(x_ref, gkv_ref, gq_ref, wk_ref, wv_ref, wq_ref, kn_ref, qn_ref,
                k_ref, v_ref, kh_ref, vh_ref, qh_ref, xkv_ref, xq_ref):
    @pl.when(pl.program_id(1) == 0)
    def _():
        x = x_ref[...]
        xhat = x * lax.rsqrt(jnp.mean(x * x, axis=-1, keepdims=True) + EPS)
        xkv_ref[...] = (xhat * gkv_ref[...]).astype(BF16)
        xq_ref[...] = (xhat * gq_ref[...]).astype(BF16)

    xkv = xkv_ref[...]
    k = _head_rmsnorm(jnp.dot(xkv, wk_ref[...], preferred_element_type=F32), kn_ref[...])
    v = jnp.dot(xkv, wv_ref[...], preferred_element_type=F32)
    q = _head_rmsnorm(jnp.dot(xq_ref[...], wq_ref[...], preferred_element_type=F32), qn_ref[...])
    k_ref[...] = k
    v_ref[...] = v
    for hh in range(kh_ref.shape[0]):
        sl = slice(hh * HEAD_DIM, (hh + 1) * HEAD_DIM)
        kh_ref[hh] = k[:, sl].astype(BF16)
        vh_ref[hh] = v[:, sl].astype(BF16)
        qh_ref[hh] = q[:, sl].astype(BF16)


def _kvq(x, gkv, gq, wkv, wq, kn, qn, tm, tn):
    t, d = x.shape
    nh = D_MODEL // tn
    hb = tn // HEAD_DIM
    tile = pl.BlockSpec((tm, tn), lambda i, j: (i, j))
    heads = pl.BlockSpec((hb, tm, HEAD_DIM), lambda i, j: (j, i, 0))
    vec = pl.BlockSpec((1, d), lambda i, j: (0, 0))
    hvec = pl.BlockSpec((1, HEAD_DIM), lambda i, j: (0, 0))
    return pl.pallas_call(
        _kvq_kernel,
        grid=(t // tm, nh),
        in_specs=[
            pl.BlockSpec((tm, d), lambda i, j: (i, 0)),
            vec, vec,
            pl.BlockSpec((d, tn), lambda i, j: (0, j)),
            pl.BlockSpec((d, tn), lambda i, j: (0, j + nh)),
            pl.BlockSpec((d, tn), lambda i, j: (0, j)),
            hvec, hvec,
        ],
        out_specs=[tile, tile, heads, heads, heads],
        out_shape=[
            jax.ShapeDtypeStruct((t, D_MODEL), F32),
            jax.ShapeDtypeStruct((t, D_MODEL), F32),
            jax.ShapeDtypeStruct((N_HEADS, t, HEAD_DIM), BF16),
            jax.ShapeDtypeStruct((N_HEADS, t, HEAD_DIM), BF16),
            jax.ShapeDtypeStruct((N_HEADS, t, HEAD_DIM), BF16),
        ],
        scratch_shapes=[pltpu.VMEM((tm, d), BF16), pltpu.VMEM((tm, d), BF16)],
        compiler_params=_params("parallel", "arbitrary"),
        name="kvq",
    )(x, gkv, gq, wkv, wkv, wq, kn, qn)


def _suffix_matrix(n):
    r = lax.broadcasted_iota(jnp.int32, (n, n), 0)
    c = lax.broadcasted_iota(jnp.int32, (n, n), 1)
    return jnp.where(r > c, 1.0, 0.0).astype(BF16)


NEG_BIG = -1e30
STICK_GONE = -110.0
LOG2E = 1.4426950408889634


def _sb_logs(zr, valid):
    scale = HEAD_DIM ** -0.5
    nz = zr * (-scale)
    tail = jnp.log(1.0 + jnp.exp2(jnp.abs(zr) * (-scale * LOG2E)))
    log_keep = jnp.minimum(nz, 0.0) - tail
    log_beta = log_keep - nz
    if valid is not None:
        log_keep = jnp.where(valid, log_keep, 0.0)
        log_beta = jnp.where(valid, log_beta, NEG_BIG)
    return log_keep.astype(BF16), log_beta, log_keep[:, 0:1]


def _attn_rows_kernel(q_ref, k_ref, v_ref, o_ref, z_ref, keep_ref, logb_ref, first_ref, later_ref,
                      w_ref, acc_ref, carry_ref, *, tq, rc, hb):
    qi = pl.program_id(1)
    tk = tq
    row0 = qi * tq
    u = _suffix_matrix(tk)
    chunks = [slice(r0, r0 + rc) for r0 in range(0, tq, rc)]
    nt = (((1,), (1,)), ((), ()))

    def attend(g, key0, nblk, diagonal):
        n = nblk * tk
        ks = pl.ds(pl.multiple_of(key0, tk), n)
        z_ref[g, :, 0:n] = lax.dot_general(q_ref[g], k_ref[g, ks, :], nt, preferred_element_type=F32)
        for blk in range(nblk - 1, -1, -1):
            cols = slice(blk * tk, (blk + 1) * tk)
            for rows in chunks:
                valid = None
                if diagonal and blk == nblk - 1:
                    row = lax.broadcasted_iota(jnp.int32, (rc, tk), 0) + rows.start
                    valid = lax.broadcasted_iota(jnp.int32, (rc, tk), 1) < row
                log_keep, log_beta, first = _sb_logs(z_ref[g, rows, cols], valid)
                keep_ref[g, rows, cols] = log_keep
                logb_ref[g, rows, cols] = log_beta
                first_ref[g, blk, rows, :] = first
            later_ref[g, :, cols] = jnp.dot(keep_ref[g, :, cols], u, preferred_element_type=F32)
            for rows in chunks:
                later = later_ref[g, rows, cols]
                carry = carry_ref[g, rows, :]
                w_ref[g, rows, cols] = jnp.exp(logb_ref[g, rows, cols] + later + carry).astype(BF16)
                carry_ref[g, rows, :] = carry + later[:, 0:1] + first_ref[g, blk, rows, :]
        acc_ref[g] += jnp.dot(w_ref[g, :, 0:n], v_ref[g, ks, :], preferred_element_type=F32)

    carry_ref[...] = jnp.zeros_like(carry_ref)
    acc_ref[...] = jnp.zeros_like(acc_ref)
    @pl.when(qi == 0)
    def _():
        for g in range(hb):
            attend(g, 0, 1, True)

    @pl.when(qi > 0)
    def _():
        key1 = row0 - tk
        for g in range(hb):
            attend(g, key1, 2, True)

        def stick_left():
            return jnp.max(carry_ref[...]) > STICK_GONE

        def body(state):
            m, _ = state
            for g in range(hb):
                attend(g, key1 - (m + 1) * tk, 1, False)
            return m + 1, stick_left()

        lax.while_loop(lambda state: (state[0] < qi - 1) & state[1], body,
                       (jnp.int32(0), stick_left()))

    for g in range(hb):
        o_ref[:, g * HEAD_DIM:(g + 1) * HEAD_DIM] = acc_ref[g].astype(o_ref.dtype)


def _attn_rows(qh, kh, vh, tq, rc, hb):
    h, t, hd = qh.shape
    assert h % hb == 0 and t % tq == 0 and t >= 2 * tq and tq % rc == 0
    resident = pl.BlockSpec((hb, t, hd), lambda hi, qi: (hi, 0, 0), pipeline_mode=pl.Buffered(1))
    return pl.pallas_call(
        functools.partial(_attn_rows_kernel, tq=tq, rc=rc, hb=hb),
        grid=(h // hb, t // tq),
        in_specs=[pl.BlockSpec((hb, tq, hd), lambda hi, qi: (hi, qi, 0)), resident, resident],
        out_specs=pl.BlockSpec((tq, hb * hd), lambda hi, qi: (qi, hi)),
        out_shape=jax.ShapeDtypeStruct((t, h * hd), BF16),
        scratch_shapes=[
            pltpu.VMEM((hb, tq, 2 * tq), F32),
            pltpu.VMEM((hb, tq, 2 * tq), BF16),
            pltpu.VMEM((hb, tq, 2 * tq), F32),
            pltpu.VMEM((hb, 2, tq, 1), F32),
            pltpu.VMEM((hb, tq, 2 * tq), F32),
            pltpu.VMEM((hb, tq, 2 * tq), BF16),
            pltpu.VMEM((hb, tq, hd), F32),
            pltpu.VMEM((hb, tq, 1), F32),
        ],
        compiler_params=_params("parallel", "arbitrary"),
        name="attn_prompt",
    )(qh, kh, vh)


def _attn_sample_kernel(q_ref, kn_ref, vn_ref, ck_hbm, cv_hbm, o_ref,
                        kbuf_ref, vbuf_ref, sem, kt_ref, vt_ref, z_ref, keep_ref, logb_ref,
                        later_ref, w_ref, acc_ref, carry_ref, *, s, tp, tk, pad, rc):
    b = pl.program_id(0)
    n_heads = q_ref.shape[0]
    nkb = ck_hbm.shape[1] // tp
    m = n_heads * s
    chunks = [slice(r0, r0 + rc) for r0 in range(0, m, rc)]
    head_rows = [slice(h * s, (h + 1) * s) for h in range(n_heads)]
    nt = (((1,), (1,)), ((), ()))
    slot = b % 2

    def cache_copies(row, blk, into):
        keys = pl.ds(pl.multiple_of((nkb - 1 - blk) * tp, tp), tp)
        return (pltpu.make_async_copy(ck_hbm.at[row, keys], kbuf_ref.at[into], sem.at[into, 0]),
                pltpu.make_async_copy(cv_hbm.at[row, keys], vbuf_ref.at[into], sem.at[into, 1]))

    def attend(n, nk, keys_of, values_of, u, masked):
        for h in range(n_heads):
            z_ref[head_rows[h], 0:n] = lax.dot_general(q_ref[h], keys_of(h), nt,
                                                      preferred_element_type=F32)
        for sub in range(n // nk - 1, -1, -1):
            cols = slice(sub * nk, (sub + 1) * nk)
            for rows in chunks:
                valid = None
                if masked:
                    row = lax.broadcasted_iota(jnp.int32, (rc, nk), 0) % s
                    valid = lax.broadcasted_iota(jnp.int32, (rc, nk), 1) < row
                log_keep, log_beta, first = _sb_logs(z_ref[rows, cols], valid)
                keep_ref[rows, 0:nk] = log_keep
                logb_ref[rows, 0:nk] = log_beta
                carry_ref[1, rows, :] = first
            later_ref[:, 0:nk] = jnp.dot(keep_ref[:, 0:nk], u, preferred_element_type=F32)
            for rows in chunks:
                later = later_ref[rows, 0:nk]
                carry = carry_ref[0, rows, :]
                w_ref[rows, cols] = jnp.exp(logb_ref[rows, 0:nk] + later + carry).astype(BF16)
                carry_ref[0, rows, :] = carry + later[:, 0:1] + carry_ref[1, rows, :]
        for h in range(n_heads):
            acc_ref[head_rows[h], :] += jnp.dot(w_ref[head_rows[h], 0:n], values_of(h),
                                                preferred_element_type=F32)

    def attend_cache_block():
        kc = 16
        for c in range(0, tp, kc):
            kt_ref[:, c:c + kc, :] = jnp.swapaxes(kbuf_ref[slot, c:c + kc], 0, 1).astype(BF16)
            vt_ref[:, c:c + kc, :] = jnp.swapaxes(vbuf_ref[slot, c:c + kc], 0, 1).astype(BF16)
        attend(tp, tk, lambda h: kt_ref[h], lambda h: vt_ref[h], _suffix_matrix(tk), False)

    @pl.when(b == 0)
    def _():
        for copy in cache_copies(0, 0, 0):
            copy.start()

    @pl.when(b + 1 < pl.num_programs(0))
    def _():
        for copy in cache_copies(b + 1, 0, 1 - slot):
            copy.start()

    carry_ref[...] = jnp.zeros_like(carry_ref)
    acc_ref[...] = jnp.zeros_like(acc_ref)
    zeros = jnp.zeros((pad - s, HEAD_DIM), BF16)
    attend(pad, pad,
           lambda h: jnp.concatenate([kn_ref[h], zeros], axis=0),
           lambda h: jnp.concatenate([vn_ref[h], zeros], axis=0),
           _suffix_matrix(pad), True)

    for copy in cache_copies(b, 0, slot):
        copy.wait()
    attend_cache_block()

    def stick_left():
        return jnp.max(carry_ref[0]) > STICK_GONE

    def older(state):
        blk, _ = state
        copies = cache_copies(b, blk, slot)
        for copy in copies:
            copy.start()
        for copy in copies:
            copy.wait()
        attend_cache_block()
        return blk + 1, stick_left()

    lax.while_loop(lambda state: (state[0] < nkb) & state[1], older, (jnp.int32(1), stick_left()))

    for h in range(n_heads):
        o_ref[:, h * HEAD_DIM:(h + 1) * HEAD_DIM] = acc_ref[head_rows[h], :].astype(o_ref.dtype)


def _attn_sample(qh, kh, vh, cache_k, cache_v, s, tp, tk, rc):
    h, t, hd = qh.shape
    bn, past = cache_k.shape[:2]
    m = h * s
    assert rc % s == 0 and m % rc == 0 and past % tp == 0 and tp % tk == 0 and hd <= tk
    new = pl.BlockSpec((h, s, hd), lambda b: (0, b, 0))
    hbm = pl.BlockSpec(memory_space=pl.ANY)
    return pl.pallas_call(
        functools.partial(_attn_sample_kernel, s=s, tp=tp, tk=tk, pad=hd, rc=rc),
        grid=(bn,),
        in_specs=[new, new, new, hbm, hbm],
        out_specs=pl.BlockSpec((s, h * hd), lambda b: (b, 0)),
        out_shape=jax.ShapeDtypeStruct((t, h * hd), BF16),
        scratch_shapes=[
            pltpu.VMEM((2, tp, h, hd), F32),
            pltpu.VMEM((2, tp, h, hd), F32),
            pltpu.SemaphoreType.DMA((2, 2)),
            pltpu.VMEM((h, tp, hd), BF16),
            pltpu.VMEM((h, tp, hd), BF16),
            pltpu.VMEM((m, tp), F32),
            pltpu.VMEM((m, tk), BF16),
            pltpu.VMEM((m, tk), F32),
            pltpu.VMEM((m, tk), F32),
            pltpu.VMEM((m, tp), BF16),
            pltpu.VMEM((m, hd), F32),
            pltpu.VMEM((2, m, 1), F32),
        ],
        compiler_params=_params("arbitrary"),
        name="attn_sample",
    )(qh, kh, vh, cache_k, cache_v)


def _trunk(x, conv_buf, h0, pos0, cache, p, tm, ts):
    bn, s, d = x.shape
    t = bn * s
    x0 = x.reshape(t, d)
    tm_wide = min(2 * tm, t)
    tn_wide = 512 if t > tm else 1024

    gate, rec = _in_proj(x0, p["a_norm"], p["a_w_in"], tm_wide, tn_wide)
    hg, h_last, new_buf = _rglru(
        rec.reshape(bn, s, d), gate.reshape(bn, s, d), conv_buf, h0.reshape(bn, 1, d),
        p["a_conv_w"], p["a_conv_b"], p["a_b_r"], p["a_b_i"], p["a_lambda"], p["a_w_r"], p["a_w_i"],
        pos0, ts, 512 if s > ts else d)
    x1 = _matmul_res(hg.reshape(t, d), p["a_w_out"], x0, tm, d)
    x2 = _mlp(x1, p["mlp_norm0"], p["mlp_w_up0"], p["mlp_w_down0"], tm_wide, tn_wide)

    k, v, kh, vh, qh = _kvq(x2, p["kv_norm"], p["b_norm"], p["w_kv"], p["b_w_q"],
                            p["k_norm"], p["b_q_norm"], tm, 512)
    if cache is None:
        o = _attn_rows(qh, kh, vh, 256, 32, 4)
    else:
        o = _attn_sample(qh, kh, vh, cache[0], cache[1], s, 512, 256, 32)
    x3 = _matmul_res(o, p["b_w_o"], x2, tm, d)
    y = _mlp(x3, p["mlp_norm1"], p["mlp_w_up1"], p["mlp_w_down1"], tm_wide, tn_wide)

    return (y.reshape(bn, s, d), h_last.reshape(1, bn, d), new_buf.reshape(1, bn, CONV_W - 1, d),
            k.reshape(bn, s, N_HEADS, HEAD_DIM), v.reshape(bn, s, N_HEADS, HEAD_DIM))


def kernel(x_prompt, x_sample, state_lru_h, state_conv, cache_k, cache_v, a_norm, a_w_in, a_conv_w, a_conv_b, a_w_r, a_b_r, a_w_i, a_b_i, a_lambda, a_w_out, kv_norm, w_kv, k_norm, b_norm, b_w_q, b_q_norm, b_w_o, mlp_norm, mlp_w_up, mlp_w_down):
    row = lambda a: a.reshape(1, -1)
    p = {
        "a_norm": row(a_norm[0]), "a_w_in": a_w_in[0].astype(BF16),
        "a_conv_w": a_conv_w[0], "a_conv_b": row(a_conv_b[0]),
        "a_w_r": a_w_r[0].astype(BF16), "a_b_r": row(a_b_r[0]),
        "a_w_i": a_w_i[0].astype(BF16), "a_b_i": row(a_b_i[0]),
        "a_lambda": row(a_lambda[0]), "a_w_out": a_w_out[0].astype(BF16),
        "kv_norm": row(kv_norm), "w_kv": w_kv.astype(BF16), "k_norm": row(k_norm),
        "b_norm": row(b_norm[0]), "b_w_q": b_w_q[0].astype(BF16), "b_q_norm": row(b_q_norm[0]),
        "b_w_o": b_w_o[0].astype(BF16),
        "mlp_norm0": row(mlp_norm[0]), "mlp_w_up0": mlp_w_up[0].astype(BF16),
        "mlp_w_down0": mlp_w_down[0].astype(BF16),
        "mlp_norm1": row(mlp_norm[1]), "mlp_w_up1": mlp_w_up[1].astype(BF16),
        "mlp_w_down1": mlp_w_down[1].astype(BF16),
    }
    bp, sp, d = x_prompt.shape
    bs, ss, _ = x_sample.shape
    zero_conv = jnp.zeros((bp, CONV_W - 1, D_RNN), x_prompt.dtype)
    zero_h = jnp.zeros((bp, D_RNN), x_prompt.dtype)
    y_p, p_h, p_conv, p_k, p_v = _trunk(x_prompt, zero_conv, zero_h, 0, None, p, 512, 256)
    cache = (cache_k, cache_v)
    y_s, s_h, s_conv, s_k, s_v = _trunk(x_sample, state_conv[0], state_lru_h[0], PAST_LEN, cache,
                                        p, 512, ss)
    return (y_p, y_s, p_h, p_conv, p_k, p_v, s_h, s_conv, s_k, s_v)
```

```python
import functools

import jax
import jax.numpy as jnp
from jax import lax
from jax.experimental import pallas as pl
from jax.experimental.pallas import tpu as pltpu

F32 = jnp.float32
BF16 = jnp.bfloat16

D_MODEL = 2048
D_RNN = D_MODEL
N_HEADS = 16
HEAD_DIM = D_MODEL // N_HEADS
N_LRU_HEADS = 16
LRU_BLOCK = D_RNN // N_LRU_HEADS
CONV_W = 4
LRU_C = 8.0
D_FF = 4 * D_MODEL
PAST_LEN = 2048
EPS = 1e-6

V7X_VMEM_LIMIT_BYTES = 56 * 1024 * 1024
SUBLANES = 8


def _params(*sem):
    return pltpu.CompilerParams(dimension_semantics=sem, vmem_limit_bytes=V7X_VMEM_LIMIT_BYTES)


def _rmsnorm(x, g):
    y = x * lax.rsqrt(jnp.mean(x * x, axis=-1, keepdims=True) + EPS)
    return y * g


def _row_chunks(n, size=128):
    return [slice(r0, min(r0 + size, n)) for r0 in range(0, n, size)]


def _softplus(x):
    return jnp.maximum(x, 0.0) + jnp.log(1.0 + jnp.exp(-jnp.abs(x)))


def _head_rmsnorm(t, g):
    outs = []
    for k in range(t.shape[1] // HEAD_DIM):
        outs.append(_rmsnorm(t[:, k * HEAD_DIM:(k + 1) * HEAD_DIM], g))
    return jnp.concatenate(outs, axis=1) if len(outs) > 1 else outs[0]


def _in_proj_kernel(x_ref, g_ref, wg_ref, wr_ref, gate_ref, rec_ref, xn_ref):
    @pl.when(pl.program_id(1) == 0)
    def _():
        for rows in _row_chunks(x_ref.shape[0]):
            xn_ref[rows, :] = _rmsnorm(x_ref[rows, :], g_ref[...]).astype(BF16)

    xn = xn_ref[...]
    gate_ref[...] = jax.nn.gelu(jnp.dot(xn, wg_ref[...], preferred_element_type=F32))
    rec_ref[...] = jnp.dot(xn, wr_ref[...], preferred_element_type=F32)


def _in_proj(x, g, w, tm, tn):
    t, d = x.shape
    nh = D_RNN // tn
    return pl.pallas_call(
        _in_proj_kernel,
        grid=(t // tm, nh),
        in_specs=[
            pl.BlockSpec((tm, d), lambda i, j: (i, 0)),
            pl.BlockSpec((1, d), lambda i, j: (0, 0)),
            pl.BlockSpec((d, tn), lambda i, j: (0, j)),
            pl.BlockSpec((d, tn), lambda i, j: (0, j + nh)),
        ],
        out_specs=[
            pl.BlockSpec((tm, tn), lambda i, j: (i, j)),
            pl.BlockSpec((tm, tn), lambda i, j: (i, j)),
        ],
        out_shape=[jax.ShapeDtypeStruct((t, D_RNN), F32)] * 2,
        scratch_shapes=[pltpu.VMEM((tm, d), BF16)],
        compiler_params=_params("parallel", "arbitrary"),
        name="in_proj",
    )(x, g, w, w)


def _rglru_kernel(rec_ref, gate_ref, cbuf_ref, h0_ref, cw_ref, cb_ref, br_ref, bi_ref, lam_ref,
                  wr_ref, wi_ref, hg_ref, hlast_ref, nbuf_ref, ext_ref, hc_ref, *, ts, tc, pos0):
    t = pl.program_id(2)
    pad = SUBLANES

    @pl.when(t == 0)
    def _():
        ext_ref[pad - (CONV_W - 1):pad, :] = cbuf_ref[...]
        hc_ref[...] = h0_ref[...]

    ext_ref[pad:pad + ts, :] = rec_ref[...]
    c = cb_ref[...] + ext_ref[pad - 3:pad - 3 + ts, :] * cw_ref[0:1, :]
    for j in range(1, CONV_W):
        c = c + ext_ref[pad - 3 + j:pad - 3 + j + ts, :] * cw_ref[j:j + 1, :]

    c16 = c.astype(BF16)
    rs, is_ = [], []
    for k in range(tc // LRU_BLOCK):
        blk = c16[:, k * LRU_BLOCK:(k + 1) * LRU_BLOCK]
        rs.append(jnp.dot(blk, wr_ref[k], preferred_element_type=F32))
        is_.append(jnp.dot(blk, wi_ref[k], preferred_element_type=F32))
    r = jax.nn.sigmoid(jnp.concatenate(rs, axis=1) + br_ref[...])
    ig = jax.nn.sigmoid(jnp.concatenate(is_, axis=1) + bi_ref[...])

    log_a = (-LRU_C * r) * _softplus(-lam_ref[...])
    a = jnp.exp(log_a)
    m2 = jnp.tanh(-log_a) * (a * a + 1.0)
    mult = m2 * lax.rsqrt(jnp.maximum(m2, jnp.finfo(F32).tiny))
    row = lax.broadcasted_iota(jnp.int32, (ts, tc), 0)
    if pos0 == 0:
        mult = jnp.where((row == 0) & (t == 0), 1.0, mult)
    b = mult * ig * c

    groups = ts // SUBLANES
    a = a.reshape(groups, SUBLANES, tc)
    b = b.reshape(groups, SUBLANES, tc)
    sub = lax.broadcasted_iota(jnp.int32, (groups, SUBLANES, tc), 1)
    d = 1
    while d < SUBLANES:
        keep = sub >= d
        a_sh = jnp.where(keep, pltpu.roll(a, d, 1), 1.0)
        b_sh = jnp.where(keep, pltpu.roll(b, d, 1), 0.0)
        b = a * b_sh + b
        a = a * a_sh
        d *= 2
    state = hc_ref[...]
    for g in range(0, groups, 2):
        h0 = b[g] + a[g] * state
        h1 = b[g + 1] + a[g + 1] * h0[SUBLANES - 1:SUBLANES, :]
        state = h1[SUBLANES - 1:SUBLANES, :]
        rows = slice(g * SUBLANES, (g + 2) * SUBLANES)
        h = jnp.concatenate([h0, h1], axis=0)
        hg_ref[rows, :] = (h * gate_ref[rows, :]).astype(hg_ref.dtype)

    hc_ref[...] = state
    hlast_ref[...] = state
    nbuf_ref[...] = ext_ref[pad + ts - (CONV_W - 1):pad + ts, :]
    ext_ref[0:pad, :] = ext_ref[ts:ts + pad, :]


def _rglru(rec, gate, cbuf, h0, cw, cb, br, bi, lam, wr, wi, pos0, ts, tc):
    bn, s, c = rec.shape
    nblk = tc // LRU_BLOCK
    seq = pl.BlockSpec((None, ts, tc), lambda b, ci, ti: (b, ti, ci))
    vec = pl.BlockSpec((1, tc), lambda b, ci, ti: (0, ci))
    return pl.pallas_call(
        functools.partial(_rglru_kernel, ts=ts, tc=tc, pos0=pos0),
        grid=(bn, c // tc, s // ts),
        in_specs=[
            seq, seq,
            pl.BlockSpec((None, CONV_W - 1, tc), lambda b, ci, ti: (b, 0, ci)),
            pl.BlockSpec((None, 1, tc), lambda b, ci, ti: (b, 0, ci)),
            pl.BlockSpec((CONV_W, tc), lambda b, ci, ti: (0, ci)),
            vec, vec, vec, vec,
            pl.BlockSpec((nblk, LRU_BLOCK, LRU_BLOCK), lambda b, ci, ti: (ci, 0, 0)),
            pl.BlockSpec((nblk, LRU_BLOCK, LRU_BLOCK), lambda b, ci, ti: (ci, 0, 0)),
        ],
        out_specs=[
            seq,
            pl.BlockSpec((None, 1, tc), lambda b, ci, ti: (b, 0, ci)),
            pl.BlockSpec((None, CONV_W - 1, tc), lambda b, ci, ti: (b, 0, ci)),
        ],
        out_shape=[
            jax.ShapeDtypeStruct((bn, s, c), BF16),
            jax.ShapeDtypeStruct((bn, 1, c), F32),
            jax.ShapeDtypeStruct((bn, CONV_W - 1, c), F32),
        ],
        scratch_shapes=[pltpu.VMEM((ts + 2 * SUBLANES, tc), F32), pltpu.VMEM((1, tc), F32)],
        compiler_params=_params("parallel", "parallel", "arbitrary"),
        name="rglru",
    )(rec, gate, cbuf, h0, cw, cb, br, bi, lam, wr, wi)


def _matmul_res_kernel(a_ref, w_ref, res_ref, o_ref):
    o_ref[...] = res_ref[...] + jnp.dot(a_ref[...], w_ref[...], preferred_element_type=F32)


def _matmul_res(a, w, res, tm, tn):
    t, k = a.shape
    n = w.shape[1]
    return pl.pallas_call(
        _matmul_res_kernel,
        grid=(t // tm, n // tn),
        in_specs=[
            pl.BlockSpec((tm, k), lambda i, j: (i, 0)),
            pl.BlockSpec((k, tn), lambda i, j: (0, j)),
            pl.BlockSpec((tm, tn), lambda i, j: (i, j)),
        ],
        out_specs=pl.BlockSpec((tm, tn), lambda i, j: (i, j)),
        out_shape=jax.ShapeDtypeStruct((t, n), F32),
        compiler_params=_params("parallel", "arbitrary"),
        name="matmul_res",
    )(a, w, res)


def _mlp_kernel(x_ref, g_ref, wu_ref, wd_ref, o_ref, xn_ref):
    @pl.when(pl.program_id(1) == 0)
    def _():
        for rows in _row_chunks(x_ref.shape[0]):
            x = x_ref[rows, :]
            xn_ref[rows, :] = _rmsnorm(x, g_ref[...]).astype(BF16)
            o_ref[rows, :] = x

    h = jnp.maximum(jnp.dot(xn_ref[...], wu_ref[...], preferred_element_type=F32), 0.0)
    o_ref[...] += jnp.dot((h * h).astype(BF16), wd_ref[...], preferred_element_type=F32)


def _mlp(x, g, wu, wd, layer, tm, tf):
    t, d = x.shape
    f = wu.shape[2]
    return pl.pallas_call(
        _mlp_kernel,
        grid=(t // tm, f // tf),
        in_specs=[
            pl.BlockSpec((tm, d), lambda i, j: (i, 0)),
            pl.BlockSpec((1, d), lambda i, j: (0, 0)),
            pl.BlockSpec((None, d, tf), lambda i, j: (layer, 0, j)),
            pl.BlockSpec((None, tf, d), lambda i, j: (layer, j, 0)),
        ],
        out_specs=pl.BlockSpec((tm, d), lambda i, j: (i, 0)),
        out_shape=jax.ShapeDtypeStruct((t, d), F32),
        scratch_shapes=[pltpu.VMEM((tm, d), BF16)],
        compiler_params=_params("parallel", "arbitrary"),
        name="mlp",
    )(x, g, wu, wd)


def _kvq_kernel(x_ref, gkv_ref, gq_ref, wk_ref, wv_ref, wq_ref, kn_ref, qn_ref,
                k_ref, v_ref, kh_ref, vh_ref, qh_ref, xkv_ref, xq_ref):
    @pl.when(pl.program_id(1) == 0)
    def _():
        for rows in _row_chunks(x_ref.shape[0]):
            x = x_ref[rows, :]
            xhat = x * lax.rsqrt(jnp.mean(x * x, axis=-1, keepdims=True) + EPS)
            xkv_ref[rows, :] = (xhat * gkv_ref[...]).astype(BF16)
            xq_ref[rows, :] = (xhat * gq_ref[...]).astype(BF16)

    xkv = xkv_ref[...]
    k = _head_rmsnorm(jnp.dot(xkv, wk_ref[...], preferred_element_type=F32), kn_ref[...])
    v = jnp.dot(xkv, wv_ref[...], preferred_element_type=F32)
    q = _head_rmsnorm(jnp.dot(xq_ref[...], wq_ref[...], preferred_element_type=F32), qn_ref[...])
    k_ref[...] = k
    v_ref[...] = v
    for hh in range(kh_ref.shape[0]):
        sl = slice(hh * HEAD_DIM, (hh + 1) * HEAD_DIM)
        kh_ref[hh] = k[:, sl].astype(BF16)
        vh_ref[hh] = v[:, sl].astype(BF16)
        qh_ref[hh] = q[:, sl].astype(BF16)


def _kvq(x, gkv, gq, wkv, wq, kn, qn, tm, tn):
    t, d = x.shape
    nh = D_MODEL // tn
    hb = tn // HEAD_DIM
    tile = pl.BlockSpec((tm, tn), lambda i, j: (i, j))
    heads = pl.BlockSpec((hb, tm, HEAD_DIM), lambda i, j: (j, i, 0))
    vec = pl.BlockSpec((1, d), lambda i, j: (0, 0))
    hvec = pl.BlockSpec((1, HEAD_DIM), lambda i, j: (0, 0))
    return pl.pallas_call(
        _kvq_kernel,
        grid=(t // tm, nh),
        in_specs=[
            pl.BlockSpec((tm, d), lambda i, j: (i, 0)),
            vec, vec,
            pl.BlockSpec((d, tn), lambda i, j: (0, j)),
            pl.BlockSpec((d, tn), lambda i, j: (0, j + nh)),
            pl.BlockSpec((d, tn), lambda i, j: (0, j)),
            hvec, hvec,
        ],
        out_specs=[tile, tile, heads, heads, heads],
        out_shape=[
            jax.ShapeDtypeStruct((t, D_MODEL), F32),
            jax.ShapeDtypeStruct((t, D_MODEL), F32),
            jax.ShapeDtypeStruct((N_HEADS, t, HEAD_DIM), BF16),
            jax.ShapeDtypeStruct((N_HEADS, t, HEAD_DIM), BF16),
            jax.ShapeDtypeStruct((N_HEADS, t, HEAD_DIM), BF16),
        ],
        scratch_shapes=[pltpu.VMEM((tm, d), BF16), pltpu.VMEM((tm, d), BF16)],
        compiler_params=_params("parallel", "arbitrary"),
        name="kvq",
    )(x, gkv, gq, wkv, wkv, wq, kn, qn)


def _suffix_matrix(n):
    r = lax.broadcasted_iota(jnp.int32, (n, n), 0)
    c = lax.broadcasted_iota(jnp.int32, (n, n), 1)
    return jnp.where(r > c, 1.0, 0.0).astype(BF16)


NEG_BIG = -1e30
STICK_GONE = -110.0
LOG2E = 1.4426950408889634


def _sb_logs(zr, valid):
    scale = HEAD_DIM ** -0.5
    nz = zr * (-scale)
    tail = jnp.log(1.0 + jnp.exp2(jnp.abs(zr) * (-scale * LOG2E)))
    log_keep = jnp.minimum(nz, 0.0) - tail
    log_beta = log_keep - nz
    if valid is not None:
        log_keep = jnp.where(valid, log_keep, 0.0)
        log_beta = jnp.where(valid, log_beta, NEG_BIG)
    return log_keep.astype(BF16), log_beta, log_keep[:, 0:1]


def _attn_rows_kernel(q_ref, k_ref, v_ref, o_ref, z_ref, keep_ref, logb_ref, first_ref, later_ref,
                      w_ref, acc_ref, carry_ref, *, tq, rc, hb):
    qi = pl.program_id(1)
    tk = tq
    row0 = qi * tq
    u = _suffix_matrix(tk)
    chunks = [slice(r0, r0 + rc) for r0 in range(0, tq, rc)]
    nt = (((1,), (1,)), ((), ()))

    def attend(g, key0, nblk, diagonal):
        n = nblk * tk
        ks = pl.ds(pl.multiple_of(key0, tk), n)
        z_ref[g, :, 0:n] = lax.dot_general(q_ref[g], k_ref[g, ks, :], nt, preferred_element_type=F32)
        for blk in range(nblk - 1, -1, -1):
            cols = slice(blk * tk, (blk + 1) * tk)
            for rows in chunks:
                valid = None
                if diagonal and blk == nblk - 1:
                    row = lax.broadcasted_iota(jnp.int32, (rc, tk), 0) + rows.start
                    valid = lax.broadcasted_iota(jnp.int32, (rc, tk), 1) < row
                log_keep, log_beta, first = _sb_logs(z_ref[g, rows, cols], valid)
                keep_ref[g, rows, cols] = log_keep
                logb_ref[g, rows, cols] = log_beta
                first_ref[g, blk, rows, :] = first
            later_ref[g, :, cols] = jnp.dot(keep_ref[g, :, cols], u, preferred_element_type=F32)
            for rows in chunks:
                later = later_ref[g, rows, cols]
                carry = carry_ref[g, rows, :]
                w_ref[g, rows, cols] = jnp.exp(logb_ref[g, rows, cols] + later + carry).astype(BF16)
                carry_ref[g, rows, :] = carry + later[:, 0:1] + first_ref[g, blk, rows, :]
        acc_ref[g] += jnp.dot(w_ref[g, :, 0:n], v_ref[g, ks, :], preferred_element_type=F32)

    carry_ref[...] = jnp.zeros_like(carry_ref)
    acc_ref[...] = jnp.zeros_like(acc_ref)
    @pl.when(qi == 0)
    def _():
        for g in range(hb):
            attend(g, 0, 1, True)

    @pl.when(qi > 0)
    def _():
        key1 = row0 - tk
        for g in range(hb):
            attend(g, key1, 2, True)

        def stick_left():
            return jnp.max(carry_ref[...]) > STICK_GONE

        def body(state):
            m, _ = state
            for g in range(hb):
                attend(g, key1 - (m + 1) * tk, 1, False)
            return m + 1, stick_left()

        lax.while_loop(lambda state: (state[0] < qi - 1) & state[1], body,
                       (jnp.int32(0), stick_left()))

    for g in range(hb):
        o_ref[:, g * HEAD_DIM:(g + 1) * HEAD_DIM] = acc_ref[g].astype(o_ref.dtype)


def _attn_rows(qh, kh, vh, tq, rc, hb):
    h, t, hd = qh.shape
    assert h % hb == 0 and t % tq == 0 and t >= 2 * tq and tq % rc == 0
    resident = pl.BlockSpec((hb, t, hd), lambda hi, qi: (hi, 0, 0), pipeline_mode=pl.Buffered(1))
    return pl.pallas_call(
        functools.partial(_attn_rows_kernel, tq=tq, rc=rc, hb=hb),
        grid=(h // hb, t // tq),
        in_specs=[pl.BlockSpec((hb, tq, hd), lambda hi, qi: (hi, qi, 0)), resident, resident],
        out_specs=pl.BlockSpec((tq, hb * hd), lambda hi, qi: (qi, hi)),
        out_shape=jax.ShapeDtypeStruct((t, h * hd), BF16),
        scratch_shapes=[
            pltpu.VMEM((hb, tq, 2 * tq), F32),
            pltpu.VMEM((hb, tq, 2 * tq), BF16),
            pltpu.VMEM((hb, tq, 2 * tq), F32),
            pltpu.VMEM((hb, 2, tq, 1), F32),
            pltpu.VMEM((hb, tq, 2 * tq), F32),
            pltpu.VMEM((hb, tq, 2 * tq), BF16),
            pltpu.VMEM((hb, tq, hd), F32),
            pltpu.VMEM((hb, tq, 1), F32),
        ],
        compiler_params=_params("parallel", "arbitrary"),
        name="attn_prompt",
    )(qh, kh, vh)


def _attn_sample_kernel(q_ref, kn_ref, vn_ref, ck_hbm, cv_hbm, o_ref,
                        kbuf_ref, vbuf_ref, sem, kt_ref, vt_ref, z_ref, keep_ref, logb_ref,
                        later_ref, w_ref, acc_ref, carry_ref, *, s, tp, tk, pad, rc):
    b = pl.program_id(0)
    n_heads = q_ref.shape[0]
    nkb = ck_hbm.shape[1] // tp
    m = n_heads * s
    chunks = [slice(r0, r0 + rc) for r0 in range(0, m, rc)]
    head_rows = [slice(h * s, (h + 1) * s) for h in range(n_heads)]
    nt = (((1,), (1,)), ((), ()))
    slot = b % 2

    def cache_copies(row, blk, into):
        keys = pl.ds(pl.multiple_of((nkb - 1 - blk) * tp, tp), tp)
        return (pltpu.make_async_copy(ck_hbm.at[row, keys], kbuf_ref.at[into], sem.at[into, 0]),
                pltpu.make_async_copy(cv_hbm.at[row, keys], vbuf_ref.at[into], sem.at[into, 1]))

    def attend(n, nk, keys_of, values_of, u, masked):
        for h in range(n_heads):
            z_ref[head_rows[h], 0:n] = lax.dot_general(q_ref[h], keys_of(h), nt,
                                                      preferred_element_type=F32)
        for sub in range(n // nk - 1, -1, -1):
            cols = slice(sub * nk, (sub + 1) * nk)
            for rows in chunks:
                valid = None
                if masked:
                    row = lax.broadcasted_iota(jnp.int32, (rc, nk), 0) % s
                    valid = lax.broadcasted_iota(jnp.int32, (rc, nk), 1) < row
                log_keep, log_beta, first = _sb_logs(z_ref[rows, cols], valid)
                keep_ref[rows, 0:nk] = log_keep
                logb_ref[rows, 0:nk] = log_beta
                carry_ref[1, rows, :] = first
            later_ref[:, 0:nk] = jnp.dot(keep_ref[:, 0:nk], u, preferred_element_type=F32)
            for rows in chunks:
                later = later_ref[rows, 0:nk]
                carry = carry_ref[0, rows, :]
                w_ref[rows, cols] = jnp.exp(logb_ref[rows, 0:nk] + later + carry).astype(BF16)
                carry_ref[0, rows, :] = carry + later[:, 0:1] + carry_ref[1, rows, :]
        for h in range(n_heads):
            acc_ref[head_rows[h], :] += jnp.dot(w_ref[head_rows[h], 0:n], values_of(h),
                                                preferred_element_type=F32)

    def attend_cache_block():
        kc = 16
        for c in range(0, tp, kc):
            kt_ref[:, c:c + kc, :] = jnp.swapaxes(kbuf_ref[slot, c:c + kc], 0, 1).astype(BF16)
            vt_ref[:, c:c + kc, :] = jnp.swapaxes(vbuf_ref[slot, c:c + kc], 0, 1).astype(BF16)
        attend(tp, tk, lambda h: kt_ref[h], lambda h: vt_ref[h], _suffix_matrix(tk), False)

    @pl.when(b == 0)
    def _():
        for copy in cache_copies(0, 0, 0):
            copy.start()

    @pl.when(b + 1 < pl.num_programs(0))
    def _():
        for copy in cache_copies(b + 1, 0, 1 - slot):
            copy.start()

    carry_ref[...] = jnp.zeros_like(carry_ref)
    acc_ref[...] = jnp.zeros_like(acc_ref)
    zeros = jnp.zeros((pad - s, HEAD_DIM), BF16)
    attend(pad, pad,
           lambda h: jnp.concatenate([kn_ref[h], zeros], axis=0),
           lambda h: jnp.concatenate([vn_ref[h], zeros], axis=0),
           _suffix_matrix(pad), True)

    for copy in cache_copies(b, 0, slot):
        copy.wait()
    attend_cache_block()

    def stick_left():
        return jnp.max(carry_ref[0]) > STICK_GONE

    def older(state):
        blk, _ = state
        copies = cache_copies(b, blk, slot)
        for copy in copies:
            copy.start()
        for copy in copies:
            copy.wait()
        attend_cache_block()
        return blk + 1, stick_left()

    lax.while_loop(lambda state: (state[0] < nkb) & state[1], older, (jnp.int32(1), stick_left()))

    for h in range(n_heads):
        o_ref[:, h * HEAD_DIM:(h + 1) * HEAD_DIM] = acc_ref[head_rows[h], :].astype(o_ref.dtype)


def _attn_sample(qh, kh, vh, cache_k, cache_v, s, tp, tk, rc):
    h, t, hd = qh.shape
    bn, past = cache_k.shape[:2]
    m = h * s
    assert rc % s == 0 and m % rc == 0 and past % tp == 0 and tp % tk == 0 and hd <= tk
    new = pl.BlockSpec((h, s, hd), lambda b: (0, b, 0))
    hbm = pl.BlockSpec(memory_space=pl.ANY)
    return pl.pallas_call(
        functools.partial(_attn_sample_kernel, s=s, tp=tp, tk=tk, pad=hd, rc=rc),
        grid=(bn,),
        in_specs=[new, new, new, hbm, hbm],
        out_specs=pl.BlockSpec((s, h * hd), lambda b: (b, 0)),
        out_shape=jax.ShapeDtypeStruct((t, h * hd), BF16),
        scratch_shapes=[
            pltpu.VMEM((2, tp, h, hd), F32),
            pltpu.VMEM((2, tp, h, hd), F32),
            pltpu.SemaphoreType.DMA((2, 2)),
            pltpu.VMEM((h, tp, hd), BF16),
            pltpu.VMEM((h, tp, hd), BF16),
            pltpu.VMEM((m, tp), F32),
            pltpu.VMEM((m, tk), BF16),
            pltpu.VMEM((m, tk), F32),
            pltpu.VMEM((m, tk), F32),
            pltpu.VMEM((m, tp), BF16),
            pltpu.VMEM((m, hd), F32),
            pltpu.VMEM((2, m, 1), F32),
        ],
        compiler_params=_params("arbitrary"),
        name="attn_sample",
    )(qh, kh, vh, cache_k, cache_v)


def _trunk(x, conv_buf, h0, pos0, cache, p, tm, ts):
    bn, s, d = x.shape
    t = bn * s
    x0 = x.reshape(t, d)
    tm_wide = min(2 * tm, t)
    tn_wide = 512 if t > tm else 1024

    gate, rec = _in_proj(x0, p["a_norm"], p["a_w_in"], tm_wide, tn_wide)
    hg, h_last, new_buf = _rglru(
        rec.reshape(bn, s, d), gate.reshape(bn, s, d), conv_buf, h0.reshape(bn, 1, d),
        p["a_conv_w"], p["a_conv_b"], p["a_b_r"], p["a_b_i"], p["a_lambda"], p["a_w_r"], p["a_w_i"],
        pos0, ts, 1024 if s > ts else d)
    x1 = _matmul_res(hg.reshape(t, d), p["a_w_out"], x0, tm, d)
    x2 = _mlp(x1, p["mlp_norm0"], p["mlp_w_up"], p["mlp_w_down"], 0, tm_wide, tn_wide)

    k, v, kh, vh, qh = _kvq(x2, p["kv_norm"], p["b_norm"], p["w_kv"], p["b_w_q"],
                            p["k_norm"], p["b_q_norm"], tm_wide, tn_wide // 2)
    if cache is None:
        o = _attn_rows(qh, kh, vh, 256, 32, 4)
    else:
        o = _attn_sample(qh, kh, vh, cache[0], cache[1], s, 512, 256, 32)
    x3 = _matmul_res(o, p["b_w_o"], x2, tm, d)
    y = _mlp(x3, p["mlp_norm1"], p["mlp_w_up"], p["mlp_w_down"], 1, tm_wide, tn_wide)

    return (y.reshape(bn, s, d), h_last.reshape(1, bn, d), new_buf.reshape(1, bn, CONV_W - 1, d),
            k.reshape(bn, s, N_HEADS, HEAD_DIM), v.reshape(bn, s, N_HEADS, HEAD_DIM))


def kernel(x_prompt, x_sample, state_lru_h, state_conv, cache_k, cache_v, a_norm, a_w_in, a_conv_w, a_conv_b, a_w_r, a_b_r, a_w_i, a_b_i, a_lambda, a_w_out, kv_norm, w_kv, k_norm, b_norm, b_w_q, b_q_norm, b_w_o, mlp_norm, mlp_w_up, mlp_w_down):
    row = lambda a: a.reshape(1, -1)
    p = {
        "a_norm": row(a_norm[0]), "a_w_in": a_w_in[0].astype(BF16),
        "a_conv_w": a_conv_w[0], "a_conv_b": row(a_conv_b[0]),
        "a_w_r": a_w_r[0].astype(BF16), "a_b_r": row(a_b_r[0]),
        "a_w_i": a_w_i[0].astype(BF16), "a_b_i": row(a_b_i[0]),
        "a_lambda": row(a_lambda[0]), "a_w_out": a_w_out[0].astype(BF16),
        "kv_norm": row(kv_norm), "w_kv": w_kv.astype(BF16), "k_norm": row(k_norm),
        "b_norm": row(b_norm[0]), "b_w_q": b_w_q[0].astype(BF16), "b_q_norm": row(b_q_norm[0]),
        "b_w_o": b_w_o[0].astype(BF16),
        "mlp_norm0": row(mlp_norm[0]), "mlp_norm1": row(mlp_norm[1]),
        "mlp_w_up": mlp_w_up.astype(BF16), "mlp_w_down": mlp_w_down.astype(BF16),
    }
    bp, sp, d = x_prompt.shape
    bs, ss, _ = x_sample.shape
    zero_conv = jnp.zeros((bp, CONV_W - 1, D_RNN), x_prompt.dtype)
    zero_h = jnp.zeros((bp, D_RNN), x_prompt.dtype)
    y_p, p_h, p_conv, p_k, p_v = _trunk(x_prompt, zero_conv, zero_h, 0, None, p, 512, 256)
    cache = (cache_k, cache_v)
    y_s, s_h, s_conv, s_k, s_v = _trunk(x_sample, state_conv[0], state_lru_h[0], PAST_LEN, cache,
                                        p, 512, ss)
    return (y_p, y_s, p_h, p_conv, p_k, p_v, s_h, s_conv, s_k, s_v)
```

```python
import functools

import jax
import jax.numpy as jnp
from jax import lax
from jax.experimental import pallas as pl
from jax.experimental.pallas import tpu as pltpu

F32 = jnp.float32
BF16 = jnp.bfloat16

D_MODEL = 2048
D_RNN = D_MODEL
N_HEADS = 16
HEAD_DIM = D_MODEL // N_HEADS
N_LRU_HEADS = 16
LRU_BLOCK = D_RNN // N_LRU_HEADS
CONV_W = 4
LRU_C = 8.0
D_FF = 4 * D_MODEL
PAST_LEN = 2048
EPS = 1e-6

V7X_VMEM_LIMIT_BYTES = 56 * 1024 * 1024
SUBLANES = 8


def _params(*sem):
    return pltpu.CompilerParams(dimension_semantics=sem, vmem_limit_bytes=V7X_VMEM_LIMIT_BYTES)


def _rmsnorm(x, g):
    y = x * lax.rsqrt(jnp.mean(x * x, axis=-1, keepdims=True) + EPS)
    return y * g


def _row_chunks(n, size=128):
    return [slice(r0, min(r0 + size, n)) for r0 in range(0, n, size)]


def _softplus(x):
    return jnp.maximum(x, 0.0) + jnp.log(1.0 + jnp.exp(-jnp.abs(x)))


def _head_rmsnorm(t, g):
    outs = []
    for k in range(t.shape[1] // HEAD_DIM):
        outs.append(_rmsnorm(t[:, k * HEAD_DIM:(k + 1) * HEAD_DIM], g))
    return jnp.concatenate(outs, axis=1) if len(outs) > 1 else outs[0]


def _in_proj_kernel(x_ref, g_ref, wg_ref, wr_ref, gate_ref, rec_ref, xn_ref):
    @pl.when(pl.program_id(1) == 0)
    def _():
        for rows in _row_chunks(x_ref.shape[0]):
            xn_ref[rows, :] = _rmsnorm(x_ref[rows, :], g_ref[...]).astype(BF16)

    xn = xn_ref[...]
    gate_ref[...] = jax.nn.gelu(jnp.dot(xn, wg_ref[...], preferred_element_type=F32))
    rec_ref[...] = jnp.dot(xn, wr_ref[...], preferred_element_type=F32)


def _in_proj(x, g, w, tm, tn):
    t, d = x.shape
    nh = D_RNN // tn
    return pl.pallas_call(
        _in_proj_kernel,
        grid=(t // tm, nh),
        in_specs=[
            pl.BlockSpec((tm, d), lambda i, j: (i, 0)),
            pl.BlockSpec((1, d), lambda i, j: (0, 0)),
            pl.BlockSpec((d, tn), lambda i, j: (0, j)),
            pl.BlockSpec((d, tn), lambda i, j: (0, j + nh)),
        ],
        out_specs=[
            pl.BlockSpec((tm, tn), lambda i, j: (i, j)),
            pl.BlockSpec((tm, tn), lambda i, j: (i, j)),
        ],
        out_shape=[jax.ShapeDtypeStruct((t, D_RNN), F32)] * 2,
        scratch_shapes=[pltpu.VMEM((tm, d), BF16)],
        compiler_params=_params("parallel", "arbitrary"),
        name="in_proj",
    )(x, g, w, w)


def _rglru_kernel(rec_ref, gate_ref, cbuf_ref, h0_ref, cw_ref, cb_ref, br_ref, bi_ref, lam_ref,
                  wr_ref, wi_ref, hg_ref, hlast_ref, nbuf_ref, ext_ref, hc_ref, *, ts, tc, pos0):
    t = pl.program_id(2)
    pad = SUBLANES

    @pl.when(t == 0)
    def _():
        ext_ref[pad - (CONV_W - 1):pad, :] = cbuf_ref[...]
        hc_ref[...] = h0_ref[...]

    ext_ref[pad:pad + ts, :] = rec_ref[...]
    c = cb_ref[...] + ext_ref[pad - 3:pad - 3 + ts, :] * cw_ref[0:1, :]
    for j in range(1, CONV_W):
        c = c + ext_ref[pad - 3 + j:pad - 3 + j + ts, :] * cw_ref[j:j + 1, :]

    c16 = c.astype(BF16)
    rs, is_ = [], []
    for k in range(tc // LRU_BLOCK):
        blk = c16[:, k * LRU_BLOCK:(k + 1) * LRU_BLOCK]
        rs.append(jnp.dot(blk, wr_ref[k], preferred_element_type=F32))
        is_.append(jnp.dot(blk, wi_ref[k], preferred_element_type=F32))
    r = jax.nn.sigmoid(jnp.concatenate(rs, axis=1) + br_ref[...])
    ig = jax.nn.sigmoid(jnp.concatenate(is_, axis=1) + bi_ref[...])

    log_a = (-LRU_C * r) * _softplus(-lam_ref[...])
    a = jnp.exp(log_a)
    m2 = jnp.tanh(-log_a) * (a * a + 1.0)
    mult = m2 * lax.rsqrt(jnp.maximum(m2, jnp.finfo(F32).tiny))
    row = lax.broadcasted_iota(jnp.int32, (ts, tc), 0)
    if pos0 == 0:
        mult = jnp.where((row == 0) & (t == 0), 1.0, mult)
    b = mult * ig * c

    groups = ts // SUBLANES
    a = a.reshape(groups, SUBLANES, tc)
    b = b.reshape(groups, SUBLANES, tc)
    sub = lax.broadcasted_iota(jnp.int32, (groups, SUBLANES, tc), 1)
    d = 1
    while d < SUBLANES:
        keep = sub >= d
        a_sh = jnp.where(keep, pltpu.roll(a, d, 1), 1.0)
        b_sh = jnp.where(keep, pltpu.roll(b, d, 1), 0.0)
        b = a * b_sh + b
        a = a * a_sh
        d *= 2
    state = hc_ref[...]
    for g in range(0, groups, 2):
        h0 = b[g] + a[g] * state
        h1 = b[g + 1] + a[g + 1] * h0[SUBLANES - 1:SUBLANES, :]
        state = h1[SUBLANES - 1:SUBLANES, :]
        rows = slice(g * SUBLANES, (g + 2) * SUBLANES)
        h = jnp.concatenate([h0, h1], axis=0)
        hg_ref[rows, :] = (h * gate_ref[rows, :]).astype(hg_ref.dtype)

    hc_ref[...] = state
    hlast_ref[...] = state
    nbuf_ref[...] = ext_ref[pad + ts - (CONV_W - 1):pad + ts, :]
    ext_ref[0:pad, :] = ext_ref[ts:ts + pad, :]


def _rglru(rec, gate, cbuf, h0, cw, cb, br, bi, lam, wr, wi, pos0, ts, tc):
    bn, s, c = rec.shape
    nblk = tc // LRU_BLOCK
    seq = pl.BlockSpec((None, ts, tc), lambda b, ci, ti: (b, ti, ci))
    vec = pl.BlockSpec((1, tc), lambda b, ci, ti: (0, ci))
    return pl.pallas_call(
        functools.partial(_rglru_kernel, ts=ts, tc=tc, pos0=pos0),
        grid=(bn, c // tc, s // ts),
        in_specs=[
            seq, seq,
            pl.BlockSpec((None, CONV_W - 1, tc), lambda b, ci, ti: (b, 0, ci)),
            pl.BlockSpec((None, 1, tc), lambda b, ci, ti: (b, 0, ci)),
            pl.BlockSpec((CONV_W, tc), lambda b, ci, ti: (0, ci)),
            vec, vec, vec, vec,
            pl.BlockSpec((nblk, LRU_BLOCK, LRU_BLOCK), lambda b, ci, ti: (ci, 0, 0)),
            pl.BlockSpec((nblk, LRU_BLOCK, LRU_BLOCK), lambda b, ci, ti: (ci, 0, 0)),
        ],
        out_specs=[
            seq,
            pl.BlockSpec((None, 1, tc), lambda b, ci, ti: (b, 0, ci)),
            pl.BlockSpec((None, CONV_W - 1, tc), lambda b, ci, ti: (b, 0, ci)),
        ],
        out_shape=[
            jax.ShapeDtypeStruct((bn, s, c), BF16),
            jax.ShapeDtypeStruct((bn, 1, c), F32),
            jax.ShapeDtypeStruct((bn, CONV_W - 1, c), F32),
        ],
        scratch_shapes=[pltpu.VMEM((ts + 2 * SUBLANES, tc), F32), pltpu.VMEM((1, tc), F32)],
        compiler_params=_params("parallel", "parallel", "arbitrary"),
        name="rglru",
    )(rec, gate, cbuf, h0, cw, cb, br, bi, lam, wr, wi)


def _matmul_res_kernel(a_ref, w_ref, res_ref, o_ref):
    o_ref[...] = res_ref[...] + jnp.dot(a_ref[...], w_ref[...], preferred_element_type=F32)


def _matmul_res(a, w, res, tm, tn):
    t, k = a.shape
    n = w.shape[1]
    return pl.pallas_call(
        _matmul_res_kernel,
        grid=(t // tm, n // tn),
        in_specs=[
            pl.BlockSpec((tm, k), lambda i, j: (i, 0)),
            pl.BlockSpec((k, tn), lambda i, j: (0, j)),
            pl.BlockSpec((tm, tn), lambda i, j: (i, j)),
        ],
        out_specs=pl.BlockSpec((tm, tn), lambda i, j: (i, j)),
        out_shape=jax.ShapeDtypeStruct((t, n), F32),
        compiler_params=_params("parallel", "arbitrary"),
        name="matmul_res",
    )(a, w, res)


def _mlp_kernel(x_ref, g_ref, wu_ref, wd_ref, o_ref, xn_ref):
    @pl.when(pl.program_id(1) == 0)
    def _():
        for rows in _row_chunks(x_ref.shape[0]):
            x = x_ref[rows, :]
            xn_ref[rows, :] = _rmsnorm(x, g_ref[...]).astype(BF16)
            o_ref[rows, :] = x

    h = jnp.maximum(jnp.dot(xn_ref[...], wu_ref[...], preferred_element_type=F32), 0.0)
    o_ref[...] += jnp.dot((h * h).astype(BF16), wd_ref[...], preferred_element_type=F32)


def _mlp(x, g, wu, wd, layer, tm, tf):
    t, d = x.shape
    f = wu.shape[2]
    return pl.pallas_call(
        _mlp_kernel,
        grid=(t // tm, f // tf),
        in_specs=[
            pl.BlockSpec((tm, d), lambda i, j: (i, 0)),
            pl.BlockSpec((1, d), lambda i, j: (0, 0)),
            pl.BlockSpec((None, d, tf), lambda i, j: (layer, 0, j)),
            pl.BlockSpec((None, tf, d), lambda i, j: (layer, j, 0)),
        ],
        out_specs=pl.BlockSpec((tm, d), lambda i, j: (i, 0)),
        out_shape=jax.ShapeDtypeStruct((t, d), F32),
        scratch_shapes=[pltpu.VMEM((tm, d), BF16)],
        compiler_params=_params("parallel", "arbitrary"),
        name="mlp",
    )(x, g, wu, wd)


def _kvq_kernel(x_ref, gkv_ref, gq_ref, wk_ref, wv_ref, wq_ref, kn_ref, qn_ref,
                k_ref, v_ref, kh_ref, vh_ref, qh_ref, xkv_ref, xq_ref):
    @pl.when(pl.program_id(1) == 0)
    def _():
        for rows in _row_chunks(x_ref.shape[0]):
            x = x_ref[rows, :]
            xhat = x * lax.rsqrt(jnp.mean(x * x, axis=-1, keepdims=True) + EPS)
            xkv_ref[rows, :] = (xhat * gkv_ref[...]).astype(BF16)
            xq_ref[rows, :] = (xhat * gq_ref[...]).astype(BF16)

    xkv = xkv_ref[...]
    k = _head_rmsnorm(jnp.dot(xkv, wk_ref[...], preferred_element_type=F32), kn_ref[...])
    v = jnp.dot(xkv, wv_ref[...], preferred_element_type=F32)
    q = _head_rmsnorm(jnp.dot(xq_ref[...], wq_ref[...], preferred_element_type=F32), qn_ref[...])
    k_ref[...] = k
    v_ref[...] = v
    for hh in range(kh_ref.shape[0]):
        sl = slice(hh * HEAD_DIM, (hh + 1) * HEAD_DIM)
        kh_ref[hh] = k[:, sl].astype(BF16)
        vh_ref[hh] = v[:, sl].astype(BF16)
        qh_ref[hh] = q[:, sl].astype(BF16)


def _kvq(x, gkv, gq, wkv, wq, kn, qn, tm, tn):
    t, d = x.shape
    nh = D_MODEL // tn
    hb = tn // HEAD_DIM
    tile = pl.BlockSpec((tm, tn), lambda i, j: (i, j))
    heads = pl.BlockSpec((hb, tm, HEAD_DIM), lambda i, j: (j, i, 0))
    vec = pl.BlockSpec((1, d), lambda i, j: (0, 0))
    hvec = pl.BlockSpec((1, HEAD_DIM), lambda i, j: (0, 0))
    return pl.pallas_call(
        _kvq_kernel,
        grid=(t // tm, nh),
        in_specs=[
            pl.BlockSpec((tm, d), lambda i, j: (i, 0)),
            vec, vec,
            pl.BlockSpec((d, tn), lambda i, j: (0, j)),
            pl.BlockSpec((d, tn), lambda i, j: (0, j + nh)),
            pl.BlockSpec((d, tn), lambda i, j: (0, j)),
            hvec, hvec,
        ],
        out_specs=[tile, tile, heads, heads, heads],
        out_shape=[
            jax.ShapeDtypeStruct((t, D_MODEL), F32),
            jax.ShapeDtypeStruct((t, D_MODEL), F32),
            jax.ShapeDtypeStruct((N_HEADS, t, HEAD_DIM), BF16),
            jax.ShapeDtypeStruct((N_HEADS, t, HEAD_DIM), BF16),
            jax.ShapeDtypeStruct((N_HEADS, t, HEAD_DIM), BF16),
        ],
        scratch_shapes=[pltpu.VMEM((tm, d), BF16), pltpu.VMEM((tm, d), BF16)],
        compiler_params=_params("parallel", "arbitrary"),
        name="kvq",
    )(x, gkv, gq, wkv, wkv, wq, kn, qn)


def _suffix_matrix(n):
    r = lax.broadcasted_iota(jnp.int32, (n, n), 0)
    c = lax.broadcasted_iota(jnp.int32, (n, n), 1)
    return jnp.where(r > c, 1.0, 0.0).astype(BF16)


NEG_BIG = -1e30
STICK_GONE = -110.0
LOG2E = 1.4426950408889634


def _sb_logs(zr, valid):
    scale = HEAD_DIM ** -0.5
    nz = zr * (-scale)
    tail = jnp.log(1.0 + jnp.exp2(jnp.abs(zr) * (-scale * LOG2E)))
    log_keep = jnp.minimum(nz, 0.0) - tail
    log_beta = log_keep - nz
    if valid is not None:
        log_keep = jnp.where(valid, log_keep, 0.0)
        log_beta = jnp.where(valid, log_beta, NEG_BIG)
    return log_keep.astype(BF16), log_beta, log_keep[:, 0:1]


def _attn_rows_kernel(q_ref, k_ref, v_ref, o_ref, z_ref, keep_ref, logb_ref, first_ref, later_ref,
                      w_ref, acc_ref, carry_ref, *, tb, nrt, rc, hb):
    qi = pl.program_id(1)
    u = _suffix_matrix(tb)
    chunks = [slice(r0, r0 + rc) for r0 in range(0, tb, rc)]
    nt = (((1,), (1,)), ((), ()))
    chains = [(g, rt) for g in range(hb) for rt in range(nrt)]

    def tile_block(rt):
        return qi * nrt + rt

    def attend(g, rt, blk0, nblk, diagonal):
        c = g * nrt + rt
        n = nblk * tb
        ks = pl.ds(pl.multiple_of(blk0 * tb, tb), n)
        q = q_ref[g, rt * tb:(rt + 1) * tb, :]
        z_ref[c, :, 0:n] = lax.dot_general(q, k_ref[g, ks, :], nt, preferred_element_type=F32)
        for blk in range(nblk - 1, -1, -1):
            cols = slice(blk * tb, (blk + 1) * tb)
            for rows in chunks:
                valid = None
                if diagonal and blk == nblk - 1:
                    row = lax.broadcasted_iota(jnp.int32, (rc, tb), 0) + rows.start
                    valid = lax.broadcasted_iota(jnp.int32, (rc, tb), 1) < row
                log_keep, log_beta, first = _sb_logs(z_ref[c, rows, cols], valid)
                keep_ref[c, rows, cols] = log_keep
                logb_ref[c, rows, cols] = log_beta
                first_ref[c, blk, rows, :] = first
            later_ref[c, :, cols] = jnp.dot(keep_ref[c, :, cols], u, preferred_element_type=F32)
            for rows in chunks:
                later = later_ref[c, rows, cols]
                carry = carry_ref[c, rows, :]
                w_ref[c, rows, cols] = jnp.exp(logb_ref[c, rows, cols] + later + carry).astype(BF16)
                carry_ref[c, rows, :] = carry + later[:, 0:1] + first_ref[c, blk, rows, :]
        acc_ref[c] += jnp.dot(w_ref[c, :, 0:n], v_ref[g, ks, :], preferred_element_type=F32)

    carry_ref[...] = jnp.zeros_like(carry_ref)
    acc_ref[...] = jnp.zeros_like(acc_ref)

    @pl.when(qi == 0)
    def _():
        for g, rt in chains:
            attend(g, rt, max(rt - 1, 0), min(rt + 1, 2), True)

    @pl.when(qi > 0)
    def _():
        for g, rt in chains:
            attend(g, rt, tile_block(rt) - 1, 2, True)

    def stick_left():
        return jnp.max(carry_ref[...]) > STICK_GONE

    def earlier(state):
        m, _ = state
        for rt in range(nrt):
            blk = tile_block(rt) - 2 - m

            @pl.when(blk >= 0)
            def _():
                for g in range(hb):
                    attend(g, rt, blk, 1, False)
        return m + 1, stick_left()

    trips = tile_block(nrt - 1) - 1
    lax.while_loop(lambda state: (state[0] < trips) & state[1], earlier,
                   (jnp.int32(0), stick_left()))

    for g, rt in chains:
        o_ref[rt * tb:(rt + 1) * tb, g * HEAD_DIM:(g + 1) * HEAD_DIM] = (
            acc_ref[g * nrt + rt].astype(o_ref.dtype))


def _attn_rows(qh, kh, vh, tb, nrt, rc, hb):
    h, t, hd = qh.shape
    tq = nrt * tb
    nc = hb * nrt
    assert h % hb == 0 and t % tq == 0 and tb % rc == 0
    resident = pl.BlockSpec((hb, t, hd), lambda hi, qi: (hi, 0, 0), pipeline_mode=pl.Buffered(1))
    return pl.pallas_call(
        functools.partial(_attn_rows_kernel, tb=tb, nrt=nrt, rc=rc, hb=hb),
        grid=(h // hb, t // tq),
        in_specs=[pl.BlockSpec((hb, tq, hd), lambda hi, qi: (hi, qi, 0)), resident, resident],
        out_specs=pl.BlockSpec((tq, hb * hd), lambda hi, qi: (qi, hi)),
        out_shape=jax.ShapeDtypeStruct((t, h * hd), BF16),
        scratch_shapes=[
            pltpu.VMEM((nc, tb, 2 * tb), F32),
            pltpu.VMEM((nc, tb, 2 * tb), BF16),
            pltpu.VMEM((nc, tb, 2 * tb), F32),
            pltpu.VMEM((nc, 2, tb, 1), F32),
            pltpu.VMEM((nc, tb, 2 * tb), F32),
            pltpu.VMEM((nc, tb, 2 * tb), BF16),
            pltpu.VMEM((nc, tb, hd), F32),
            pltpu.VMEM((nc, tb, 1), F32),
        ],
        compiler_params=_params("parallel", "arbitrary"),
        name="attn_prompt",
    )(qh, kh, vh)


def _attn_sample_kernel(q_ref, kn_ref, vn_ref, ck_hbm, cv_hbm, o_ref,
                        kbuf_ref, vbuf_ref, sem, kt_ref, vt_ref, z_ref, keep_ref, logb_ref,
                        later_ref, w_ref, acc_ref, carry_ref, *, s, tp, tk, pad, rc):
    b = pl.program_id(0)
    n_heads = q_ref.shape[0]
    nkb = ck_hbm.shape[1] // tp
    m = n_heads * s
    chunks = [slice(r0, r0 + rc) for r0 in range(0, m, rc)]
    head_rows = [slice(h * s, (h + 1) * s) for h in range(n_heads)]
    nt = (((1,), (1,)), ((), ()))
    slot = b % 2

    def cache_copies(row, blk, into):
        keys = pl.ds(pl.multiple_of((nkb - 1 - blk) * tp, tp), tp)
        return (pltpu.make_async_copy(ck_hbm.at[row, keys], kbuf_ref.at[into], sem.at[into, 0]),
                pltpu.make_async_copy(cv_hbm.at[row, keys], vbuf_ref.at[into], sem.at[into, 1]))

    def attend(n, nk, keys_of, values_of, u, masked):
        for h in range(n_heads):
            z_ref[head_rows[h], 0:n] = lax.dot_general(q_ref[h], keys_of(h), nt,
                                                      preferred_element_type=F32)
        for sub in range(n // nk - 1, -1, -1):
            cols = slice(sub * nk, (sub + 1) * nk)
            for rows in chunks:
                valid = None
                if masked:
                    row = lax.broadcasted_iota(jnp.int32, (rc, nk), 0) % s
                    valid = lax.broadcasted_iota(jnp.int32, (rc, nk), 1) < row
                log_keep, log_beta, first = _sb_logs(z_ref[rows, cols], valid)
                keep_ref[rows, 0:nk] = log_keep
                logb_ref[rows, 0:nk] = log_beta
                carry_ref[1, rows, :] = first
            later_ref[:, 0:nk] = jnp.dot(keep_ref[:, 0:nk], u, preferred_element_type=F32)
            for rows in chunks:
                later = later_ref[rows, 0:nk]
                carry = carry_ref[0, rows, :]
                w_ref[rows, cols] = jnp.exp(logb_ref[rows, 0:nk] + later + carry).astype(BF16)
                carry_ref[0, rows, :] = carry + later[:, 0:1] + carry_ref[1, rows, :]
        for h in range(n_heads):
            acc_ref[head_rows[h], :] += jnp.dot(w_ref[head_rows[h], 0:n], values_of(h),
                                                preferred_element_type=F32)

    def attend_cache_block():
        kc = 16
        for c in range(0, tp, kc):
            kt_ref[:, c:c + kc, :] = jnp.swapaxes(kbuf_ref[slot, c:c + kc], 0, 1).astype(BF16)
            vt_ref[:, c:c + kc, :] = jnp.swapaxes(vbuf_ref[slot, c:c + kc], 0, 1).astype(BF16)
        attend(tp, tk, lambda h: kt_ref[h], lambda h: vt_ref[h], _suffix_matrix(tk), False)

    @pl.when(b == 0)
    def _():
        for copy in cache_copies(0, 0, 0):
            copy.start()

    @pl.when(b + 1 < pl.num_programs(0))
    def _():
        for copy in cache_copies(b + 1, 0, 1 - slot):
            copy.start()

    carry_ref[...] = jnp.zeros_like(carry_ref)
    acc_ref[...] = jnp.zeros_like(acc_ref)
    zeros = jnp.zeros((pad - s, HEAD_DIM), BF16)
    attend(pad, pad,
           lambda h: jnp.concatenate([kn_ref[h], zeros], axis=0),
           lambda h: jnp.concatenate([vn_ref[h], zeros], axis=0),
           _suffix_matrix(pad), True)

    for copy in cache_copies(b, 0, slot):
        copy.wait()
    attend_cache_block()

    def stick_left():
        return jnp.max(carry_ref[0]) > STICK_GONE

    def older(state):
        blk, _ = state
        copies = cache_copies(b, blk, slot)
        for copy in copies:
            copy.start()
        for copy in copies:
            copy.wait()
        attend_cache_block()
        return blk + 1, stick_left()

    lax.while_loop(lambda state: (state[0] < nkb) & state[1], older, (jnp.int32(1), stick_left()))

    for h in range(n_heads):
        o_ref[:, h * HEAD_DIM:(h + 1) * HEAD_DIM] = acc_ref[head_rows[h], :].astype(o_ref.dtype)


def _attn_sample(qh, kh, vh, cache_k, cache_v, s, tp, tk, rc):
    h, t, hd = qh.shape
    bn, past = cache_k.shape[:2]
    m = h * s
    assert rc % s == 0 and m % rc == 0 and past % tp == 0 and tp % tk == 0 and hd <= tk
    new = pl.BlockSpec((h, s, hd), lambda b: (0, b, 0))
    hbm = pl.BlockSpec(memory_space=pl.ANY)
    return pl.pallas_call(
        functools.partial(_attn_sample_kernel, s=s, tp=tp, tk=tk, pad=hd, rc=rc),
        grid=(bn,),
        in_specs=[new, new, new, hbm, hbm],
        out_specs=pl.BlockSpec((s, h * hd), lambda b: (b, 0)),
        out_shape=jax.ShapeDtypeStruct((t, h * hd), BF16),
        scratch_shapes=[
            pltpu.VMEM((2, tp, h, hd), F32),
            pltpu.VMEM((2, tp, h, hd), F32),
            pltpu.SemaphoreType.DMA((2, 2)),
            pltpu.VMEM((h, tp, hd), BF16),
            pltpu.VMEM((h, tp, hd), BF16),
            pltpu.VMEM((m, tp), F32),
            pltpu.VMEM((m, tk), BF16),
            pltpu.VMEM((m, tk), F32),
            pltpu.VMEM((m, tk), F32),
            pltpu.VMEM((m, tp), BF16),
            pltpu.VMEM((m, hd), F32),
            pltpu.VMEM((2, m, 1), F32),
        ],
        compiler_params=_params("arbitrary"),
        name="attn_sample",
    )(qh, kh, vh, cache_k, cache_v)


def _trunk(x, conv_buf, h0, pos0, cache, p, tm, ts):
    bn, s, d = x.shape
    t = bn * s
    x0 = x.reshape(t, d)
    tm_wide = min(2 * tm, t)
    tn_wide = 512 if t > tm else 1024

    gate, rec = _in_proj(x0, p["a_norm"], p["a_w_in"], tm_wide, tn_wide)
    hg, h_last, new_buf = _rglru(
        rec.reshape(bn, s, d), gate.reshape(bn, s, d), conv_buf, h0.reshape(bn, 1, d),
        p["a_conv_w"], p["a_conv_b"], p["a_b_r"], p["a_b_i"], p["a_lambda"], p["a_w_r"], p["a_w_i"],
        pos0, ts, 1024 if s > ts else d)
    x1 = _matmul_res(hg.reshape(t, d), p["a_w_out"], x0, tm, d)
    x2 = _mlp(x1, p["mlp_norm0"], p["mlp_w_up"], p["mlp_w_down"], 0, tm_wide, tn_wide)

    k, v, kh, vh, qh = _kvq(x2, p["kv_norm"], p["b_norm"], p["w_kv"], p["b_w_q"],
                            p["k_norm"], p["b_q_norm"], tm_wide, tn_wide // 2)
    if cache is None:
        o = _attn_rows(qh, kh, vh, 256, 4, 32, 2)
    else:
        o = _attn_sample(qh, kh, vh, cache[0], cache[1], s, 256, 256, 32)
    x3 = _matmul_res(o, p["b_w_o"], x2, tm, d)
    y = _mlp(x3, p["mlp_norm1"], p["mlp_w_up"], p["mlp_w_down"], 1, tm_wide, tn_wide)

    return (y.reshape(bn, s, d), h_last.reshape(1, bn, d), new_buf.reshape(1, bn, CONV_W - 1, d),
            k.reshape(bn, s, N_HEADS, HEAD_DIM), v.reshape(bn, s, N_HEADS, HEAD_DIM))


def kernel(x_prompt, x_sample, state_lru_h, state_conv, cache_k, cache_v, a_norm, a_w_in, a_conv_w, a_conv_b, a_w_r, a_b_r, a_w_i, a_b_i, a_lambda, a_w_out, kv_norm, w_kv, k_norm, b_norm, b_w_q, b_q_norm, b_w_o, mlp_norm, mlp_w_up, mlp_w_down):
    row = lambda a: a.reshape(1, -1)
    p = {
        "a_norm": row(a_norm[0]), "a_w_in": a_w_in[0].astype(BF16),
        "a_conv_w": a_conv_w[0], "a_conv_b": row(a_conv_b[0]),
        "a_w_r": a_w_r[0].astype(BF16), "a_b_r": row(a_b_r[0]),
        "a_w_i": a_w_i[0].astype(BF16), "a_b_i": row(a_b_i[0]),
        "a_lambda": row(a_lambda[0]), "a_w_out": a_w_out[0].astype(BF16),
        "kv_norm": row(kv_norm), "w_kv": w_kv.astype(BF16), "k_norm": row(k_norm),
        "b_norm": row(b_norm[0]), "b_w_q": b_w_q[0].astype(BF16), "b_q_norm": row(b_q_norm[0]),
        "b_w_o": b_w_o[0].astype(BF16),
        "mlp_norm0": row(mlp_norm[0]), "mlp_norm1": row(mlp_norm[1]),
        "mlp_w_up": mlp_w_up.astype(BF16), "mlp_w_down": mlp_w_down.astype(BF16),
    }
    bp, sp, d = x_prompt.shape
    bs, ss, _ = x_sample.shape
    zero_conv = jnp.zeros((bp, CONV_W - 1, D_RNN), x_prompt.dtype)
    zero_h = jnp.zeros((bp, D_RNN), x_prompt.dtype)
    y_p, p_h, p_conv, p_k, p_v = _trunk(x_prompt, zero_conv, zero_h, 0, None, p, 512, 256)
    cache = (cache_k, cache_v)
    y_s, s_h, s_conv, s_k, s_v = _trunk(x_sample, state_conv[0], state_lru_h[0], PAST_LEN, cache,
                                        p, 512, ss)
    return (y_p, y_s, p_h, p_conv, p_k, p_v, s_h, s_conv, s_k, s_v)
```

```python
import functools

import jax
import jax.numpy as jnp
from jax import lax
from jax.experimental import pallas as pl
from jax.experimental.pallas import tpu as pltpu

F32 = jnp.float32
BF16 = jnp.bfloat16

D_MODEL = 2048
D_RNN = D_MODEL
N_HEADS = 16
HEAD_DIM = D_MODEL // N_HEADS
N_LRU_HEADS = 16
LRU_BLOCK = D_RNN // N_LRU_HEADS
CONV_W = 4
LRU_C = 8.0
D_FF = 4 * D_MODEL
PAST_LEN = 2048
EPS = 1e-6

V7X_VMEM_LIMIT_BYTES = 56 * 1024 * 1024
SUBLANES = 8


def _params(*sem):
    return pltpu.CompilerParams(dimension_semantics=sem, vmem_limit_bytes=V7X_VMEM_LIMIT_BYTES)


def _rmsnorm(x, g):
    y = x * lax.rsqrt(jnp.mean(x * x, axis=-1, keepdims=True) + EPS)
    return y * g


def _row_chunks(n, size=128):
    return [slice(r0, min(r0 + size, n)) for r0 in range(0, n, size)]


def _softplus(x):
    return jnp.maximum(x, 0.0) + jnp.log(1.0 + jnp.exp(-jnp.abs(x)))


def _head_rmsnorm(t, g):
    outs = []
    for k in range(t.shape[1] // HEAD_DIM):
        outs.append(_rmsnorm(t[:, k * HEAD_DIM:(k + 1) * HEAD_DIM], g))
    return jnp.concatenate(outs, axis=1) if len(outs) > 1 else outs[0]


def _in_proj_kernel(x_ref, g_ref, wg_ref, wr_ref, gate_ref, rec_ref, xn_ref):
    @pl.when(pl.program_id(1) == 0)
    def _():
        for rows in _row_chunks(x_ref.shape[0]):
            xn_ref[rows, :] = _rmsnorm(x_ref[rows, :], g_ref[...]).astype(BF16)

    xn = xn_ref[...]
    gate_ref[...] = jax.nn.gelu(jnp.dot(xn, wg_ref[...], preferred_element_type=F32))
    rec_ref[...] = jnp.dot(xn, wr_ref[...], preferred_element_type=F32)


def _in_proj(x, g, w, tm, tn):
    t, d = x.shape
    nh = D_RNN // tn
    return pl.pallas_call(
        _in_proj_kernel,
        grid=(t // tm, nh),
        in_specs=[
            pl.BlockSpec((tm, d), lambda i, j: (i, 0)),
            pl.BlockSpec((1, d), lambda i, j: (0, 0)),
            pl.BlockSpec((d, tn), lambda i, j: (0, j)),
            pl.BlockSpec((d, tn), lambda i, j: (0, j + nh)),
        ],
        out_specs=[
            pl.BlockSpec((tm, tn), lambda i, j: (i, j)),
            pl.BlockSpec((tm, tn), lambda i, j: (i, j)),
        ],
        out_shape=[jax.ShapeDtypeStruct((t, D_RNN), F32)] * 2,
        scratch_shapes=[pltpu.VMEM((tm, d), BF16)],
        compiler_params=_params("parallel", "arbitrary"),
        name="in_proj",
    )(x, g, w, w)


def _rglru_kernel(rec_ref, gate_ref, cbuf_ref, h0_ref, cw_ref, cb_ref, br_ref, bi_ref, lam_ref,
                  wr_ref, wi_ref, hg_ref, hlast_ref, nbuf_ref, ext_ref, hc_ref, *, ts, tc, pos0):
    t = pl.program_id(2)
    pad = SUBLANES

    @pl.when(t == 0)
    def _():
        ext_ref[pad - (CONV_W - 1):pad, :] = cbuf_ref[...]
        hc_ref[...] = h0_ref[...]

    ext_ref[pad:pad + ts, :] = rec_ref[...]
    c = cb_ref[...] + ext_ref[pad - 3:pad - 3 + ts, :] * cw_ref[0:1, :]
    for j in range(1, CONV_W):
        c = c + ext_ref[pad - 3 + j:pad - 3 + j + ts, :] * cw_ref[j:j + 1, :]

    c16 = c.astype(BF16)
    rs, is_ = [], []
    for k in range(tc // LRU_BLOCK):
        blk = c16[:, k * LRU_BLOCK:(k + 1) * LRU_BLOCK]
        rs.append(jnp.dot(blk, wr_ref[k], preferred_element_type=F32))
        is_.append(jnp.dot(blk, wi_ref[k], preferred_element_type=F32))
    r = jax.nn.sigmoid(jnp.concatenate(rs, axis=1) + br_ref[...])
    ig = jax.nn.sigmoid(jnp.concatenate(is_, axis=1) + bi_ref[...])

    log_a = (-LRU_C * r) * _softplus(-lam_ref[...])
    a = jnp.exp(log_a)
    m2 = jnp.tanh(-log_a) * (a * a + 1.0)
    mult = m2 * lax.rsqrt(jnp.maximum(m2, jnp.finfo(F32).tiny))
    row = lax.broadcasted_iota(jnp.int32, (ts, tc), 0)
    if pos0 == 0:
        mult = jnp.where((row == 0) & (t == 0), 1.0, mult)
    b = mult * ig * c

    groups = ts // SUBLANES
    a = a.reshape(groups, SUBLANES, tc)
    b = b.reshape(groups, SUBLANES, tc)
    sub = lax.broadcasted_iota(jnp.int32, (groups, SUBLANES, tc), 1)
    d = 1
    while d < SUBLANES:
        keep = sub >= d
        a_sh = jnp.where(keep, pltpu.roll(a, d, 1), 1.0)
        b_sh = jnp.where(keep, pltpu.roll(b, d, 1), 0.0)
        b = a * b_sh + b
        a = a * a_sh
        d *= 2
    state = hc_ref[...]
    for g in range(0, groups, 2):
        h0 = b[g] + a[g] * state
        h1 = b[g + 1] + a[g + 1] * h0[SUBLANES - 1:SUBLANES, :]
        state = h1[SUBLANES - 1:SUBLANES, :]
        rows = slice(g * SUBLANES, (g + 2) * SUBLANES)
        h = jnp.concatenate([h0, h1], axis=0)
        hg_ref[rows, :] = (h * gate_ref[rows, :]).astype(hg_ref.dtype)

    hc_ref[...] = state
    hlast_ref[...] = state
    nbuf_ref[...] = ext_ref[pad + ts - (CONV_W - 1):pad + ts, :]
    ext_ref[0:pad, :] = ext_ref[ts:ts + pad, :]


def _rglru(rec, gate, cbuf, h0, cw, cb, br, bi, lam, wr, wi, pos0, ts, tc):
    bn, s, c = rec.shape
    nblk = tc // LRU_BLOCK
    seq = pl.BlockSpec((None, ts, tc), lambda b, ci, ti: (b, ti, ci))
    vec = pl.BlockSpec((1, tc), lambda b, ci, ti: (0, ci))
    return pl.pallas_call(
        functools.partial(_rglru_kernel, ts=ts, tc=tc, pos0=pos0),
        grid=(bn, c // tc, s // ts),
        in_specs=[
            seq, seq,
            pl.BlockSpec((None, CONV_W - 1, tc), lambda b, ci, ti: (b, 0, ci)),
            pl.BlockSpec((None, 1, tc), lambda b, ci, ti: (b, 0, ci)),
            pl.BlockSpec((CONV_W, tc), lambda b, ci, ti: (0, ci)),
            vec, vec, vec, vec,
            pl.BlockSpec((nblk, LRU_BLOCK, LRU_BLOCK), lambda b, ci, ti: (ci, 0, 0)),
            pl.BlockSpec((nblk, LRU_BLOCK, LRU_BLOCK), lambda b, ci, ti: (ci, 0, 0)),
        ],
        out_specs=[
            seq,
            pl.BlockSpec((None, 1, tc), lambda b, ci, ti: (b, 0, ci)),
            pl.BlockSpec((None, CONV_W - 1, tc), lambda b, ci, ti: (b, 0, ci)),
        ],
        out_shape=[
            jax.ShapeDtypeStruct((bn, s, c), BF16),
            jax.ShapeDtypeStruct((bn, 1, c), F32),
            jax.ShapeDtypeStruct((bn, CONV_W - 1, c), F32),
        ],
        scratch_shapes=[pltpu.VMEM((ts + 2 * SUBLANES, tc), F32), pltpu.VMEM((1, tc), F32)],
        compiler_params=_params("parallel", "parallel", "arbitrary"),
        name="rglru",
    )(rec, gate, cbuf, h0, cw, cb, br, bi, lam, wr, wi)


def _matmul_res_kernel(a_ref, w_ref, res_ref, o_ref):
    o_ref[...] = res_ref[...] + jnp.dot(a_ref[...], w_ref[...], preferred_element_type=F32)


def _matmul_res(a, w, res, tm, tn):
    t, k = a.shape
    n = w.shape[1]
    return pl.pallas_call(
        _matmul_res_kernel,
        grid=(t // tm, n // tn),
        in_specs=[
            pl.BlockSpec((tm, k), lambda i, j: (i, 0)),
            pl.BlockSpec((k, tn), lambda i, j: (0, j)),
            pl.BlockSpec((tm, tn), lambda i, j: (i, j)),
        ],
        out_specs=pl.BlockSpec((tm, tn), lambda i, j: (i, j)),
        out_shape=jax.ShapeDtypeStruct((t, n), F32),
        compiler_params=_params("parallel", "arbitrary"),
        name="matmul_res",
    )(a, w, res)


def _mlp_kernel(x_ref, g_ref, wu_ref, wd_ref, o_ref, xn_ref):
    @pl.when(pl.program_id(1) == 0)
    def _():
        for rows in _row_chunks(x_ref.shape[0]):
            x = x_ref[rows, :]
            xn_ref[rows, :] = _rmsnorm(x, g_ref[...]).astype(BF16)
            o_ref[rows, :] = x

    h = jnp.maximum(jnp.dot(xn_ref[...], wu_ref[...], preferred_element_type=F32), 0.0)
    o_ref[...] += jnp.dot((h * h).astype(BF16), wd_ref[...], preferred_element_type=F32)


def _mlp(x, g, wu, wd, layer, tm, tf):
    t, d = x.shape
    f = wu.shape[2]
    return pl.pallas_call(
        _mlp_kernel,
        grid=(t // tm, f // tf),
        in_specs=[
            pl.BlockSpec((tm, d), lambda i, j: (i, 0)),
            pl.BlockSpec((1, d), lambda i, j: (0, 0)),
            pl.BlockSpec((None, d, tf), lambda i, j: (layer, 0, j)),
            pl.BlockSpec((None, tf, d), lambda i, j: (layer, j, 0)),
        ],
        out_specs=pl.BlockSpec((tm, d), lambda i, j: (i, 0)),
        out_shape=jax.ShapeDtypeStruct((t, d), F32),
        scratch_shapes=[pltpu.VMEM((tm, d), BF16)],
        compiler_params=_params("parallel", "arbitrary"),
        name="mlp",
    )(x, g, wu, wd)


def _kvq_kernel(x_ref, gkv_ref, gq_ref, wk_ref, wv_ref, wq_ref, kn_ref, qn_ref,
                k_ref, v_ref, kh_ref, vh_ref, qh_ref, xkv_ref, xq_ref):
    @pl.when(pl.program_id(1) == 0)
    def _():
        for rows in _row_chunks(x_ref.shape[0]):
            x = x_ref[rows, :]
            xhat = x * lax.rsqrt(jnp.mean(x * x, axis=-1, keepdims=True) + EPS)
            xkv_ref[rows, :] = (xhat * gkv_ref[...]).astype(BF16)
            xq_ref[rows, :] = (xhat * gq_ref[...]).astype(BF16)

    xkv = xkv_ref[...]
    k = _head_rmsnorm(jnp.dot(xkv, wk_ref[...], preferred_element_type=F32), kn_ref[...])
    v = jnp.dot(xkv, wv_ref[...], preferred_element_type=F32)
    q = _head_rmsnorm(jnp.dot(xq_ref[...], wq_ref[...], preferred_element_type=F32), qn_ref[...])
    k_ref[...] = k
    v_ref[...] = v
    for hh in range(kh_ref.shape[0]):
        sl = slice(hh * HEAD_DIM, (hh + 1) * HEAD_DIM)
        kh_ref[hh] = k[:, sl].astype(BF16)
        vh_ref[hh] = v[:, sl].astype(BF16)
        qh_ref[hh] = q[:, sl].astype(BF16)


def _kvq(x, gkv, gq, wkv, wq, kn, qn, tm, tn):
    t, d = x.shape
    nh = D_MODEL // tn
    hb = tn // HEAD_DIM
    tile = pl.BlockSpec((tm, tn), lambda i, j: (i, j))
    heads = pl.BlockSpec((hb, tm, HEAD_DIM), lambda i, j: (j, i, 0))
    vec = pl.BlockSpec((1, d), lambda i, j: (0, 0))
    hvec = pl.BlockSpec((1, HEAD_DIM), lambda i, j: (0, 0))
    return pl.pallas_call(
        _kvq_kernel,
        grid=(t // tm, nh),
        in_specs=[
            pl.BlockSpec((tm, d), lambda i, j: (i, 0)),
            vec, vec,
            pl.BlockSpec((d, tn), lambda i, j: (0, j)),
            pl.BlockSpec((d, tn), lambda i, j: (0, j + nh)),
            pl.BlockSpec((d, tn), lambda i, j: (0, j)),
            hvec, hvec,
        ],
        out_specs=[tile, tile, heads, heads, heads],
        out_shape=[
            jax.ShapeDtypeStruct((t, D_MODEL), F32),
            jax.ShapeDtypeStruct((t, D_MODEL), F32),
            jax.ShapeDtypeStruct((N_HEADS, t, HEAD_DIM), BF16),
            jax.ShapeDtypeStruct((N_HEADS, t, HEAD_DIM), BF16),
            jax.ShapeDtypeStruct((N_HEADS, t, HEAD_DIM), BF16),
        ],
        scratch_shapes=[pltpu.VMEM((tm, d), BF16), pltpu.VMEM((tm, d), BF16)],
        compiler_params=_params("parallel", "arbitrary"),
        name="kvq",
    )(x, gkv, gq, wkv, wkv, wq, kn, qn)


def _suffix_matrix(n):
    r = lax.broadcasted_iota(jnp.int32, (n, n), 0)
    c = lax.broadcasted_iota(jnp.int32, (n, n), 1)
    return jnp.where(r > c, 1.0, 0.0).astype(BF16)


NEG_BIG = -1e30
STICK_GONE = -110.0
LOG2E = 1.4426950408889634


def _sb_logs(zr, valid):
    scale = HEAD_DIM ** -0.5
    nz = zr * (-scale)
    tail = jnp.log(1.0 + jnp.exp2(jnp.abs(zr) * (-scale * LOG2E)))
    log_keep = jnp.minimum(nz, 0.0) - tail
    log_beta = log_keep - nz
    if valid is not None:
        log_keep = jnp.where(valid, log_keep, 0.0)
        log_beta = jnp.where(valid, log_beta, NEG_BIG)
    return log_keep.astype(BF16), log_beta, log_keep[:, 0:1]


def _attn_rows_kernel(q_ref, k_ref, v_ref, o_ref, z_ref, keep_ref, logb_ref, first_ref, later_ref,
                      w_ref, acc_ref, carry_ref, *, tb, nrt, rc, hb):
    qi = pl.program_id(1)
    u = _suffix_matrix(tb)
    chunks = [slice(r0, r0 + rc) for r0 in range(0, tb, rc)]
    nt = (((1,), (1,)), ((), ()))
    chains = [(g, rt) for g in range(hb) for rt in range(nrt)]

    def tile_block(rt):
        return qi * nrt + rt

    def attend(g, rt, blk0, nblk, diagonal):
        c = g * nrt + rt
        n = nblk * tb
        ks = pl.ds(pl.multiple_of(blk0 * tb, tb), n)
        q = q_ref[g, rt * tb:(rt + 1) * tb, :]
        z_ref[c, :, 0:n] = lax.dot_general(q, k_ref[g, ks, :], nt, preferred_element_type=F32)
        for blk in range(nblk - 1, -1, -1):
            cols = slice(blk * tb, (blk + 1) * tb)
            for rows in chunks:
                valid = None
                if diagonal and blk == nblk - 1:
                    row = lax.broadcasted_iota(jnp.int32, (rc, tb), 0) + rows.start
                    valid = lax.broadcasted_iota(jnp.int32, (rc, tb), 1) < row
                log_keep, log_beta, first = _sb_logs(z_ref[c, rows, cols], valid)
                keep_ref[c, rows, cols] = log_keep
                logb_ref[c, rows, cols] = log_beta
                first_ref[c, blk, rows, :] = first
            later_ref[c, :, cols] = jnp.dot(keep_ref[c, :, cols], u, preferred_element_type=F32)
            for rows in chunks:
                later = later_ref[c, rows, cols]
                carry = carry_ref[c, rows, :]
                w_ref[c, rows, cols] = jnp.exp(logb_ref[c, rows, cols] + later + carry).astype(BF16)
                carry_ref[c, rows, :] = carry + later[:, 0:1] + first_ref[c, blk, rows, :]
        acc_ref[c] += jnp.dot(w_ref[c, :, 0:n], v_ref[g, ks, :], preferred_element_type=F32)

    carry_ref[...] = jnp.zeros_like(carry_ref)
    acc_ref[...] = jnp.zeros_like(acc_ref)

    @pl.when(qi == 0)
    def _():
        for g, rt in chains:
            attend(g, rt, max(rt - 1, 0), min(rt + 1, 2), True)

    @pl.when(qi > 0)
    def _():
        for g, rt in chains:
            attend(g, rt, tile_block(rt) - 1, 2, True)

    def stick_left():
        return jnp.max(carry_ref[...]) > STICK_GONE

    def earlier(state):
        m, _ = state
        for rt in range(nrt):
            blk = tile_block(rt) - 2 - m

            @pl.when(blk >= 0)
            def _():
                for g in range(hb):
                    attend(g, rt, blk, 1, False)
        return m + 1, stick_left()

    trips = tile_block(nrt - 1) - 1
    lax.while_loop(lambda state: (state[0] < trips) & state[1], earlier,
                   (jnp.int32(0), stick_left()))

    for g, rt in chains:
        o_ref[rt * tb:(rt + 1) * tb, g * HEAD_DIM:(g + 1) * HEAD_DIM] = (
            acc_ref[g * nrt + rt].astype(o_ref.dtype))


def _attn_rows(qh, kh, vh, tb, nrt, rc, hb):
    h, t, hd = qh.shape
    tq = nrt * tb
    nc = hb * nrt
    assert h % hb == 0 and t % tq == 0 and tb % rc == 0
    resident = pl.BlockSpec((hb, t, hd), lambda hi, qi: (hi, 0, 0), pipeline_mode=pl.Buffered(1))
    return pl.pallas_call(
        functools.partial(_attn_rows_kernel, tb=tb, nrt=nrt, rc=rc, hb=hb),
        grid=(h // hb, t // tq),
        in_specs=[pl.BlockSpec((hb, tq, hd), lambda hi, qi: (hi, qi, 0)), resident, resident],
        out_specs=pl.BlockSpec((tq, hb * hd), lambda hi, qi: (qi, hi)),
        out_shape=jax.ShapeDtypeStruct((t, h * hd), BF16),
        scratch_shapes=[
            pltpu.VMEM((nc, tb, 2 * tb), F32),
            pltpu.VMEM((nc, tb, 2 * tb), BF16),
            pltpu.VMEM((nc, tb, 2 * tb), F32),
            pltpu.VMEM((nc, 2, tb, 1), F32),
            pltpu.VMEM((nc, tb, 2 * tb), F32),
            pltpu.VMEM((nc, tb, 2 * tb), BF16),
            pltpu.VMEM((nc, tb, hd), F32),
            pltpu.VMEM((nc, tb, 1), F32),
        ],
        compiler_params=_params("parallel", "arbitrary"),
        name="attn_prompt",
    )(qh, kh, vh)


def _attn_sample_kernel(q_ref, kn_ref, vn_ref, ck_hbm, cv_hbm, o_ref,
                        kbuf_ref, vbuf_ref, sem, kt_ref, vt_ref, z_ref, keep_ref, logb_ref,
                        later_ref, w_ref, acc_ref, carry_ref, *, s, tp, tk, pad, rc):
    b = pl.program_id(0)
    n_heads = q_ref.shape[0]
    nkb = ck_hbm.shape[1] // tp
    m = n_heads * s
    chunks = [slice(r0, r0 + rc) for r0 in range(0, m, rc)]
    head_rows = [slice(h * s, (h + 1) * s) for h in range(n_heads)]
    nt = (((1,), (1,)), ((), ()))
    slot = b % 2

    def cache_copies(row, blk, into):
        keys = pl.ds(pl.multiple_of((nkb - 1 - blk) * tp, tp), tp)
        return (pltpu.make_async_copy(ck_hbm.at[row, keys], kbuf_ref.at[into], sem.at[into, 0]),
                pltpu.make_async_copy(cv_hbm.at[row, keys], vbuf_ref.at[into], sem.at[into, 1]))

    def attend(n, nk, keys_of, values_of, u, masked):
        for h in range(n_heads):
            z_ref[head_rows[h], 0:n] = lax.dot_general(q_ref[h], keys_of(h), nt,
                                                      preferred_element_type=F32)
        for sub in range(n // nk - 1, -1, -1):
            cols = slice(sub * nk, (sub + 1) * nk)
            for rows in chunks:
                valid = None
                if masked:
                    row = lax.broadcasted_iota(jnp.int32, (rc, nk), 0) % s
                    valid = lax.broadcasted_iota(jnp.int32, (rc, nk), 1) < row
                log_keep, log_beta, first = _sb_logs(z_ref[rows, cols], valid)
                keep_ref[rows, 0:nk] = log_keep
                logb_ref[rows, 0:nk] = log_beta
                carry_ref[1, rows, :] = first
            later_ref[:, 0:nk] = jnp.dot(keep_ref[:, 0:nk], u, preferred_element_type=F32)
            for rows in chunks:
                later = later_ref[rows, 0:nk]
                carry = carry_ref[0, rows, :]
                w_ref[rows, cols] = jnp.exp(logb_ref[rows, 0:nk] + later + carry).astype(BF16)
                carry_ref[0, rows, :] = carry + later[:, 0:1] + carry_ref[1, rows, :]
        for h in range(n_heads):
            acc_ref[head_rows[h], :] += jnp.dot(w_ref[head_rows[h], 0:n], values_of(h),
                                                preferred_element_type=F32)

    def attend_cache_block():
        kc = 16
        for c in range(0, tp, kc):
            kt_ref[:, c:c + kc, :] = jnp.swapaxes(kbuf_ref[slot, c:c + kc], 0, 1).astype(BF16)
            vt_ref[:, c:c + kc, :] = jnp.swapaxes(vbuf_ref[slot, c:c + kc], 0, 1).astype(BF16)
        attend(tp, tk, lambda h: kt_ref[h], lambda h: vt_ref[h], _suffix_matrix(tk), False)

    @pl.when(b == 0)
    def _():
        for copy in cache_copies(0, 0, 0):
            copy.start()

    @pl.when(b + 1 < pl.num_programs(0))
    def _():
        for copy in cache_copies(b + 1, 0, 1 - slot):
            copy.start()

    carry_ref[...] = jnp.zeros_like(carry_ref)
    acc_ref[...] = jnp.zeros_like(acc_ref)
    zeros = jnp.zeros((pad - s, HEAD_DIM), BF16)
    attend(pad, pad,
           lambda h: jnp.concatenate([kn_ref[h], zeros], axis=0),
           lambda h: jnp.concatenate([vn_ref[h], zeros], axis=0),
           _suffix_matrix(pad), True)

    for copy in cache_copies(b, 0, slot):
        copy.wait()
    attend_cache_block()

    def stick_left():
        return jnp.max(carry_ref[0]) > STICK_GONE

    def older(state):
        blk, _ = state
        copies = cache_copies(b, blk, slot)
        for copy in copies:
            copy.start()
        for copy in copies:
            copy.wait()
        attend_cache_block()
        return blk + 1, stick_left()

    lax.while_loop(lambda state: (state[0] < nkb) & state[1], older, (jnp.int32(1), stick_left()))

    for h in range(n_heads):
        o_ref[:, h * HEAD_DIM:(h + 1) * HEAD_DIM] = acc_ref[head_rows[h], :].astype(o_ref.dtype)


def _attn_sample(qh, kh, vh, cache_k, cache_v, s, tp, tk, rc):
    h, t, hd = qh.shape
    bn, past = cache_k.shape[:2]
    m = h * s
    assert rc % s == 0 and m % rc == 0 and past % tp == 0 and tp % tk == 0 and hd <= tk
    new = pl.BlockSpec((h, s, hd), lambda b: (0, b, 0))
    hbm = pl.BlockSpec(memory_space=pl.ANY)
    return pl.pallas_call(
        functools.partial(_attn_sample_kernel, s=s, tp=tp, tk=tk, pad=hd, rc=rc),
        grid=(bn,),
        in_specs=[new, new, new, hbm, hbm],
        out_specs=pl.BlockSpec((s, h * hd), lambda b: (b, 0)),
        out_shape=jax.ShapeDtypeStruct((t, h * hd), BF16),
        scratch_shapes=[
            pltpu.VMEM((2, tp, h, hd), F32),
            pltpu.VMEM((2, tp, h, hd), F32),
            pltpu.SemaphoreType.DMA((2, 2)),
            pltpu.VMEM((h, tp, hd), BF16),
            pltpu.VMEM((h, tp, hd), BF16),
            pltpu.VMEM((m, tp), F32),
            pltpu.VMEM((m, tk), BF16),
            pltpu.VMEM((m, tk), F32),
            pltpu.VMEM((m, tk), F32),
            pltpu.VMEM((m, tp), BF16),
            pltpu.VMEM((m, hd), F32),
            pltpu.VMEM((2, m, 1), F32),
        ],
        compiler_params=_params("arbitrary"),
        name="attn_sample",
    )(qh, kh, vh, cache_k, cache_v)


def _trunk(x, conv_buf, h0, pos0, cache, p, tm, ts):
    bn, s, d = x.shape
    t = bn * s
    x0 = x.reshape(t, d)
    tm_wide = min(2 * tm, t)
    tn_wide = 512 if t > tm else 1024

    gate, rec = _in_proj(x0, p["a_norm"], p["a_w_in"], tm_wide, tn_wide)
    hg, h_last, new_buf = _rglru(
        rec.reshape(bn, s, d), gate.reshape(bn, s, d), conv_buf, h0.reshape(bn, 1, d),
        p["a_conv_w"], p["a_conv_b"], p["a_b_r"], p["a_b_i"], p["a_lambda"], p["a_w_r"], p["a_w_i"],
        pos0, ts, 1024 if s > ts else d)
    x1 = _matmul_res(hg.reshape(t, d), p["a_w_out"], x0, tm, d)
    x2 = _mlp(x1, p["mlp_norm0"], p["mlp_w_up"], p["mlp_w_down"], 0, tm_wide, tn_wide)

    k, v, kh, vh, qh = _kvq(x2, p["kv_norm"], p["b_norm"], p["w_kv"], p["b_w_q"],
                            p["k_norm"], p["b_q_norm"], tm_wide, tn_wide // 2)
    if cache is None:
        o = _attn_rows(qh, kh, vh, 256, 1, 32, 4)
    else:
        o = _attn_sample(qh, kh, vh, cache[0], cache[1], s, 256, 256, 32)
    x3 = _matmul_res(o, p["b_w_o"], x2, tm, d)
    y = _mlp(x3, p["mlp_norm1"], p["mlp_w_up"], p["mlp_w_down"], 1, tm_wide, tn_wide)

    return (y.reshape(bn, s, d), h_last.reshape(1, bn, d), new_buf.reshape(1, bn, CONV_W - 1, d),
            k.reshape(bn, s, N_HEADS, HEAD_DIM), v.reshape(bn, s, N_HEADS, HEAD_DIM))


def kernel(x_prompt, x_sample, state_lru_h, state_conv, cache_k, cache_v, a_norm, a_w_in, a_conv_w, a_conv_b, a_w_r, a_b_r, a_w_i, a_b_i, a_lambda, a_w_out, kv_norm, w_kv, k_norm, b_norm, b_w_q, b_q_norm, b_w_o, mlp_norm, mlp_w_up, mlp_w_down):
    row = lambda a: a.reshape(1, -1)
    p = {
        "a_norm": row(a_norm[0]), "a_w_in": a_w_in[0].astype(BF16),
        "a_conv_w": a_conv_w[0], "a_conv_b": row(a_conv_b[0]),
        "a_w_r": a_w_r[0].astype(BF16), "a_b_r": row(a_b_r[0]),
        "a_w_i": a_w_i[0].astype(BF16), "a_b_i": row(a_b_i[0]),
        "a_lambda": row(a_lambda[0]), "a_w_out": a_w_out[0].astype(BF16),
        "kv_norm": row(kv_norm), "w_kv": w_kv.astype(BF16), "k_norm": row(k_norm),
        "b_norm": row(b_norm[0]), "b_w_q": b_w_q[0].astype(BF16), "b_q_norm": row(b_q_norm[0]),
        "b_w_o": b_w_o[0].astype(BF16),
        "mlp_norm0": row(mlp_norm[0]), "mlp_norm1": row(mlp_norm[1]),
        "mlp_w_up": mlp_w_up.astype(BF16), "mlp_w_down": mlp_w_down.astype(BF16),
    }
    bp, sp, d = x_prompt.shape
    bs, ss, _ = x_sample.shape
    zero_conv = jnp.zeros((bp, CONV_W - 1, D_RNN), x_prompt.dtype)
    zero_h = jnp.zeros((bp, D_RNN), x_prompt.dtype)
    y_p, p_h, p_conv, p_k, p_v = _trunk(x_prompt, zero_conv, zero_h, 0, None, p, 512, 256)
    cache = (cache_k, cache_v)
    y_s, s_h, s_conv, s_k, s_v = _trunk(x_sample, state_conv[0], state_lru_h[0], PAST_LEN, cache,
                                        p, 512, ss)
    return (y_p, y_s, p_h, p_conv, p_k, p_v, s_h, s_conv, s_k, s_v)
```

```python
import functools
from typing import NamedTuple

import jax
import jax.numpy as jnp
from jax import lax
from jax.experimental import pallas as pl
from jax.experimental.pallas import tpu as pltpu

F32 = jnp.float32
BF16 = jnp.bfloat16

D_MODEL = 2048
D_RNN = D_MODEL
N_HEADS = 16
HEAD_DIM = D_MODEL // N_HEADS
N_LRU_HEADS = 16
LRU_BLOCK = D_RNN // N_LRU_HEADS
CONV_W = 4
LRU_C = 8.0
D_FF = 4 * D_MODEL
PAST_LEN = 2048
EPS = 1e-6

V7X_VMEM_LIMIT_BYTES = 56 * 1024 * 1024
SUBLANES = 8

ROW_TILE = 512
SCAN_TILE = 256
ATTN_TILE = 256
ATTN_HEADS = 4
ATTN_TILES = 1
CACHE_BLOCK = 256
CHUNK_ROWS = 32


class _Tiles(NamedTuple):
    rows: int
    rows_wide: int
    cols: int
    scan_rows: int
    scan_cols: int


def _tiles(t, s):
    one_row_tile = t <= ROW_TILE
    one_time_tile = s <= SCAN_TILE
    return _Tiles(rows=min(ROW_TILE, t), rows_wide=min(2 * ROW_TILE, t),
                  cols=1024 if one_row_tile else 512,
                  scan_rows=min(SCAN_TILE, s), scan_cols=D_RNN if one_time_tile else 1024)


def _params(*sem):
    return pltpu.CompilerParams(dimension_semantics=sem, vmem_limit_bytes=V7X_VMEM_LIMIT_BYTES)


def _rmsnorm(x, g):
    y = x * lax.rsqrt(jnp.mean(x * x, axis=-1, keepdims=True) + EPS)
    return y * g


def _row_chunks(n, size=128):
    return [slice(r0, min(r0 + size, n)) for r0 in range(0, n, size)]


def _softplus(x):
    return jnp.maximum(x, 0.0) + jnp.log(1.0 + jnp.exp(-jnp.abs(x)))


def _head_rmsnorm(t, g):
    outs = []
    for k in range(t.shape[1] // HEAD_DIM):
        outs.append(_rmsnorm(t[:, k * HEAD_DIM:(k + 1) * HEAD_DIM], g))
    return jnp.concatenate(outs, axis=1) if len(outs) > 1 else outs[0]


def _in_proj_kernel(x_ref, g_ref, wg_ref, wr_ref, gate_ref, rec_ref, xn_ref):
    @pl.when(pl.program_id(1) == 0)
    def _():
        for rows in _row_chunks(x_ref.shape[0]):
            xn_ref[rows, :] = _rmsnorm(x_ref[rows, :], g_ref[...]).astype(BF16)

    xn = xn_ref[...]
    gate_ref[...] = jax.nn.gelu(jnp.dot(xn, wg_ref[...], preferred_element_type=F32))
    rec_ref[...] = jnp.dot(xn, wr_ref[...], preferred_element_type=F32)


def _in_proj(x, g, w, tm, tn):
    t, d = x.shape
    nh = D_RNN // tn
    return pl.pallas_call(
        _in_proj_kernel,
        grid=(t // tm, nh),
        in_specs=[
            pl.BlockSpec((tm, d), lambda i, j: (i, 0)),
            pl.BlockSpec((1, d), lambda i, j: (0, 0)),
            pl.BlockSpec((d, tn), lambda i, j: (0, j)),
            pl.BlockSpec((d, tn), lambda i, j: (0, j + nh)),
        ],
        out_specs=[
            pl.BlockSpec((tm, tn), lambda i, j: (i, j)),
            pl.BlockSpec((tm, tn), lambda i, j: (i, j)),
        ],
        out_shape=[jax.ShapeDtypeStruct((t, D_RNN), F32)] * 2,
        scratch_shapes=[pltpu.VMEM((tm, d), BF16)],
        compiler_params=_params("parallel", "arbitrary"),
        name="in_proj",
    )(x, g, w, w)


def _rglru_kernel(rec_ref, gate_ref, cbuf_ref, h0_ref, cw_ref, cb_ref, br_ref, bi_ref, lam_ref,
                  wr_ref, wi_ref, hg_ref, hlast_ref, nbuf_ref, ext_ref, hc_ref, *, ts, tc, pos0):
    t = pl.program_id(2)
    pad = SUBLANES

    @pl.when(t == 0)
    def _():
        ext_ref[pad - (CONV_W - 1):pad, :] = cbuf_ref[...]
        hc_ref[...] = h0_ref[...]

    ext_ref[pad:pad + ts, :] = rec_ref[...]
    c = cb_ref[...] + ext_ref[pad - 3:pad - 3 + ts, :] * cw_ref[0:1, :]
    for j in range(1, CONV_W):
        c = c + ext_ref[pad - 3 + j:pad - 3 + j + ts, :] * cw_ref[j:j + 1, :]

    c16 = c.astype(BF16)
    rs, is_ = [], []
    for k in range(tc // LRU_BLOCK):
        blk = c16[:, k * LRU_BLOCK:(k + 1) * LRU_BLOCK]
        rs.append(jnp.dot(blk, wr_ref[k], preferred_element_type=F32))
        is_.append(jnp.dot(blk, wi_ref[k], preferred_element_type=F32))
    r = jax.nn.sigmoid(jnp.concatenate(rs, axis=1) + br_ref[...])
    ig = jax.nn.sigmoid(jnp.concatenate(is_, axis=1) + bi_ref[...])

    log_a = (-LRU_C * r) * _softplus(-lam_ref[...])
    a = jnp.exp(log_a)
    m2 = jnp.tanh(-log_a) * (a * a + 1.0)
    mult = m2 * lax.rsqrt(jnp.maximum(m2, jnp.finfo(F32).tiny))
    row = lax.broadcasted_iota(jnp.int32, (ts, tc), 0)
    if pos0 == 0:
        mult = jnp.where((row == 0) & (t == 0), 1.0, mult)
    b = mult * ig * c

    groups = ts // SUBLANES
    a = a.reshape(groups, SUBLANES, tc)
    b = b.reshape(groups, SUBLANES, tc)
    sub = lax.broadcasted_iota(jnp.int32, (groups, SUBLANES, tc), 1)
    d = 1
    while d < SUBLANES:
        keep = sub >= d
        a_sh = jnp.where(keep, pltpu.roll(a, d, 1), 1.0)
        b_sh = jnp.where(keep, pltpu.roll(b, d, 1), 0.0)
        b = a * b_sh + b
        a = a * a_sh
        d *= 2
    state = hc_ref[...]
    for g in range(0, groups, 2):
        h0 = b[g] + a[g] * state
        h1 = b[g + 1] + a[g + 1] * h0[SUBLANES - 1:SUBLANES, :]
        state = h1[SUBLANES - 1:SUBLANES, :]
        rows = slice(g * SUBLANES, (g + 2) * SUBLANES)
        h = jnp.concatenate([h0, h1], axis=0)
        hg_ref[rows, :] = (h * gate_ref[rows, :]).astype(hg_ref.dtype)

    hc_ref[...] = state
    hlast_ref[...] = state
    nbuf_ref[...] = ext_ref[pad + ts - (CONV_W - 1):pad + ts, :]
    ext_ref[0:pad, :] = ext_ref[ts:ts + pad, :]


def _rglru(rec, gate, cbuf, h0, cw, cb, br, bi, lam, wr, wi, pos0, ts, tc):
    bn, s, c = rec.shape
    nblk = tc // LRU_BLOCK
    seq = pl.BlockSpec((None, ts, tc), lambda b, ci, ti: (b, ti, ci))
    vec = pl.BlockSpec((1, tc), lambda b, ci, ti: (0, ci))
    return pl.pallas_call(
        functools.partial(_rglru_kernel, ts=ts, tc=tc, pos0=pos0),
        grid=(bn, c // tc, s // ts),
        in_specs=[
            seq, seq,
            pl.BlockSpec((None, CONV_W - 1, tc), lambda b, ci, ti: (b, 0, ci)),
            pl.BlockSpec((None, 1, tc), lambda b, ci, ti: (b, 0, ci)),
            pl.BlockSpec((CONV_W, tc), lambda b, ci, ti: (0, ci)),
            vec, vec, vec, vec,
            pl.BlockSpec((nblk, LRU_BLOCK, LRU_BLOCK), lambda b, ci, ti: (ci, 0, 0)),
            pl.BlockSpec((nblk, LRU_BLOCK, LRU_BLOCK), lambda b, ci, ti: (ci, 0, 0)),
        ],
        out_specs=[
            seq,
            pl.BlockSpec((None, 1, tc), lambda b, ci, ti: (b, 0, ci)),
            pl.BlockSpec((None, CONV_W - 1, tc), lambda b, ci, ti: (b, 0, ci)),
        ],
        out_shape=[
            jax.ShapeDtypeStruct((bn, s, c), BF16),
            jax.ShapeDtypeStruct((bn, 1, c), F32),
            jax.ShapeDtypeStruct((bn, CONV_W - 1, c), F32),
        ],
        scratch_shapes=[pltpu.VMEM((ts + 2 * SUBLANES, tc), F32), pltpu.VMEM((1, tc), F32)],
        compiler_params=_params("parallel", "parallel", "arbitrary"),
        name="rglru",
    )(rec, gate, cbuf, h0, cw, cb, br, bi, lam, wr, wi)


def _matmul_res_kernel(a_ref, w_ref, res_ref, o_ref):
    o_ref[...] = res_ref[...] + jnp.dot(a_ref[...], w_ref[...], preferred_element_type=F32)


def _matmul_res(a, w, res, tm, tn):
    t, k = a.shape
    n = w.shape[1]
    return pl.pallas_call(
        _matmul_res_kernel,
        grid=(t // tm, n // tn),
        in_specs=[
            pl.BlockSpec((tm, k), lambda i, j: (i, 0)),
            pl.BlockSpec((k, tn), lambda i, j: (0, j)),
            pl.BlockSpec((tm, tn), lambda i, j: (i, j)),
        ],
        out_specs=pl.BlockSpec((tm, tn), lambda i, j: (i, j)),
        out_shape=jax.ShapeDtypeStruct((t, n), F32),
        compiler_params=_params("parallel", "arbitrary"),
        name="matmul_res",
    )(a, w, res)


def _mlp_kernel(x_ref, g_ref, wu_ref, wd_ref, o_ref, xn_ref):
    @pl.when(pl.program_id(1) == 0)
    def _():
        for rows in _row_chunks(x_ref.shape[0]):
            x = x_ref[rows, :]
            xn_ref[rows, :] = _rmsnorm(x, g_ref[...]).astype(BF16)
            o_ref[rows, :] = x

    h = jnp.maximum(jnp.dot(xn_ref[...], wu_ref[...], preferred_element_type=F32), 0.0)
    o_ref[...] += jnp.dot((h * h).astype(BF16), wd_ref[...], preferred_element_type=F32)


def _mlp(x, g, wu, wd, layer, tm, tf):
    t, d = x.shape
    f = wu.shape[2]
    return pl.pallas_call(
        _mlp_kernel,
        grid=(t // tm, f // tf),
        in_specs=[
            pl.BlockSpec((tm, d), lambda i, j: (i, 0)),
            pl.BlockSpec((1, d), lambda i, j: (0, 0)),
            pl.BlockSpec((None, d, tf), lambda i, j: (layer, 0, j)),
            pl.BlockSpec((None, tf, d), lambda i, j: (layer, j, 0)),
        ],
        out_specs=pl.BlockSpec((tm, d), lambda i, j: (i, 0)),
        out_shape=jax.ShapeDtypeStruct((t, d), F32),
        scratch_shapes=[pltpu.VMEM((tm, d), BF16)],
        compiler_params=_params("parallel", "arbitrary"),
        name="mlp",
    )(x, g, wu, wd)


def _kvq_kernel(x_ref, gkv_ref, gq_ref, wk_ref, wv_ref, wq_ref, kn_ref, qn_ref,
                k_ref, v_ref, kh_ref, vh_ref, qh_ref, xkv_ref, xq_ref):
    @pl.when(pl.program_id(1) == 0)
    def _():
        for rows in _row_chunks(x_ref.shape[0]):
            x = x_ref[rows, :]
            xhat = x * lax.rsqrt(jnp.mean(x * x, axis=-1, keepdims=True) + EPS)
            xkv_ref[rows, :] = (xhat * gkv_ref[...]).astype(BF16)
            xq_ref[rows, :] = (xhat * gq_ref[...]).astype(BF16)

    xkv = xkv_ref[...]
    k = _head_rmsnorm(jnp.dot(xkv, wk_ref[...], preferred_element_type=F32), kn_ref[...])
    v = jnp.dot(xkv, wv_ref[...], preferred_element_type=F32)
    q = _head_rmsnorm(jnp.dot(xq_ref[...], wq_ref[...], preferred_element_type=F32), qn_ref[...])
    k_ref[...] = k
    v_ref[...] = v
    for hh in range(kh_ref.shape[0]):
        sl = slice(hh * HEAD_DIM, (hh + 1) * HEAD_DIM)
        kh_ref[hh] = k[:, sl].astype(BF16)
        vh_ref[hh] = v[:, sl].astype(BF16)
        qh_ref[hh] = q[:, sl].astype(BF16)


def _kvq(x, gkv, gq, wkv, wq, kn, qn, tm, tn):
    t, d = x.shape
    nh = D_MODEL // tn
    hb = tn // HEAD_DIM
    tile = pl.BlockSpec((tm, tn), lambda i, j: (i, j))
    heads = pl.BlockSpec((hb, tm, HEAD_DIM), lambda i, j: (j, i, 0))
    vec = pl.BlockSpec((1, d), lambda i, j: (0, 0))
    hvec = pl.BlockSpec((1, HEAD_DIM), lambda i, j: (0, 0))
    return pl.pallas_call(
        _kvq_kernel,
        grid=(t // tm, nh),
        in_specs=[
            pl.BlockSpec((tm, d), lambda i, j: (i, 0)),
            vec, vec,
            pl.BlockSpec((d, tn), lambda i, j: (0, j)),
            pl.BlockSpec((d, tn), lambda i, j: (0, j + nh)),
            pl.BlockSpec((d, tn), lambda i, j: (0, j)),
            hvec, hvec,
        ],
        out_specs=[tile, tile, heads, heads, heads],
        out_shape=[
            jax.ShapeDtypeStruct((t, D_MODEL), F32),
            jax.ShapeDtypeStruct((t, D_MODEL), F32),
            jax.ShapeDtypeStruct((N_HEADS, t, HEAD_DIM), BF16),
            jax.ShapeDtypeStruct((N_HEADS, t, HEAD_DIM), BF16),
            jax.ShapeDtypeStruct((N_HEADS, t, HEAD_DIM), BF16),
        ],
        scratch_shapes=[pltpu.VMEM((tm, d), BF16), pltpu.VMEM((tm, d), BF16)],
        compiler_params=_params("parallel", "arbitrary"),
        name="kvq",
    )(x, gkv, gq, wkv, wkv, wq, kn, qn)


def _suffix_matrix(n):
    r = lax.broadcasted_iota(jnp.int32, (n, n), 0)
    c = lax.broadcasted_iota(jnp.int32, (n, n), 1)
    return jnp.where(r > c, 1.0, 0.0).astype(BF16)


NEG_BIG = -1e30
STICK_GONE = -110.0
LOG2E = 1.4426950408889634


def _sb_logs(zr, valid):
    scale = HEAD_DIM ** -0.5
    nz = zr * (-scale)
    tail = jnp.log(1.0 + jnp.exp2(jnp.abs(zr) * (-scale * LOG2E)))
    log_keep = jnp.minimum(nz, 0.0) - tail
    log_beta = log_keep - nz
    if valid is not None:
        log_keep = jnp.where(valid, log_keep, 0.0)
        log_beta = jnp.where(valid, log_beta, NEG_BIG)
    return log_keep.astype(BF16), log_beta, log_keep[:, 0:1]


def _attn_rows_kernel(q_ref, k_ref, v_ref, o_ref, z_ref, keep_ref, logb_ref, first_ref, later_ref,
                      w_ref, acc_ref, carry_ref, *, tb, nrt, rc, hb):
    qi = pl.program_id(1)
    u = _suffix_matrix(tb)
    chunks = [slice(r0, r0 + rc) for r0 in range(0, tb, rc)]
    nt = (((1,), (1,)), ((), ()))
    chains = [(g, rt) for g in range(hb) for rt in range(nrt)]

    def tile_block(rt):
        return qi * nrt + rt

    def attend(g, rt, blk0, nblk, diagonal):
        c = g * nrt + rt
        n = nblk * tb
        ks = pl.ds(pl.multiple_of(blk0 * tb, tb), n)
        q = q_ref[g, rt * tb:(rt + 1) * tb, :]
        z_ref[c, :, 0:n] = lax.dot_general(q, k_ref[g, ks, :], nt, preferred_element_type=F32)
        for blk in range(nblk - 1, -1, -1):
            cols = slice(blk * tb, (blk + 1) * tb)
            for rows in chunks:
                valid = None
                if diagonal and blk == nblk - 1:
                    row = lax.broadcasted_iota(jnp.int32, (rc, tb), 0) + rows.start
                    valid = lax.broadcasted_iota(jnp.int32, (rc, tb), 1) < row
                log_keep, log_beta, first = _sb_logs(z_ref[c, rows, cols], valid)
                keep_ref[c, rows, cols] = log_keep
                logb_ref[c, rows, cols] = log_beta
                first_ref[c, blk, rows, :] = first
            later_ref[c, :, cols] = jnp.dot(keep_ref[c, :, cols], u, preferred_element_type=F32)
            for rows in chunks:
                later = later_ref[c, rows, cols]
                carry = carry_ref[c, rows, :]
                w_ref[c, rows, cols] = jnp.exp(logb_ref[c, rows, cols] + later + carry).astype(BF16)
                carry_ref[c, rows, :] = carry + later[:, 0:1] + first_ref[c, blk, rows, :]
        acc_ref[c] += jnp.dot(w_ref[c, :, 0:n], v_ref[g, ks, :], preferred_element_type=F32)

    carry_ref[...] = jnp.zeros_like(carry_ref)
    acc_ref[...] = jnp.zeros_like(acc_ref)

    @pl.when(qi == 0)
    def _():
        for g, rt in chains:
            attend(g, rt, max(rt - 1, 0), min(rt + 1, 2), True)

    @pl.when(qi > 0)
    def _():
        for g, rt in chains:
            attend(g, rt, tile_block(rt) - 1, 2, True)

    def stick_left():
        return jnp.max(carry_ref[...]) > STICK_GONE

    def earlier(state):
        m, _ = state
        for rt in range(nrt):
            blk = tile_block(rt) - 2 - m

            @pl.when(blk >= 0)
            def _():
                for g in range(hb):
                    attend(g, rt, blk, 1, False)
        return m + 1, stick_left()

    trips = tile_block(nrt - 1) - 1
    lax.while_loop(lambda state: (state[0] < trips) & state[1], earlier,
                   (jnp.int32(0), stick_left()))

    for g, rt in chains:
        o_ref[rt * tb:(rt + 1) * tb, g * HEAD_DIM:(g + 1) * HEAD_DIM] = (
            acc_ref[g * nrt + rt].astype(o_ref.dtype))


def _attn_rows(qh, kh, vh, tb, nrt, rc, hb):
    h, t, hd = qh.shape
    tq = nrt * tb
    nc = hb * nrt
    assert h % hb == 0 and t % tq == 0 and tb % rc == 0
    resident = pl.BlockSpec((hb, t, hd), lambda hi, qi: (hi, 0, 0), pipeline_mode=pl.Buffered(1))
    return pl.pallas_call(
        functools.partial(_attn_rows_kernel, tb=tb, nrt=nrt, rc=rc, hb=hb),
        grid=(h // hb, t // tq),
        in_specs=[pl.BlockSpec((hb, tq, hd), lambda hi, qi: (hi, qi, 0)), resident, resident],
        out_specs=pl.BlockSpec((tq, hb * hd), lambda hi, qi: (qi, hi)),
        out_shape=jax.ShapeDtypeStruct((t, h * hd), BF16),
        scratch_shapes=[
            pltpu.VMEM((nc, tb, 2 * tb), F32),
            pltpu.VMEM((nc, tb, 2 * tb), BF16),
            pltpu.VMEM((nc, tb, 2 * tb), F32),
            pltpu.VMEM((nc, 2, tb, 1), F32),
            pltpu.VMEM((nc, tb, 2 * tb), F32),
            pltpu.VMEM((nc, tb, 2 * tb), BF16),
            pltpu.VMEM((nc, tb, hd), F32),
            pltpu.VMEM((nc, tb, 1), F32),
        ],
        compiler_params=_params("parallel", "arbitrary"),
        name="attn_prompt",
    )(qh, kh, vh)


def _attn_sample_kernel(q_ref, kn_ref, vn_ref, ck_hbm, cv_hbm, o_ref,
                        kbuf_ref, vbuf_ref, sem, kt_ref, vt_ref, z_ref, keep_ref, logb_ref,
                        later_ref, w_ref, acc_ref, carry_ref, *, s, tp, tk, pad, rc):
    b = pl.program_id(0)
    n_heads = q_ref.shape[0]
    nkb = ck_hbm.shape[1] // tp
    m = n_heads * s
    chunks = [slice(r0, r0 + rc) for r0 in range(0, m, rc)]
    head_rows = [slice(h * s, (h + 1) * s) for h in range(n_heads)]
    nt = (((1,), (1,)), ((), ()))
    slot = b % 2

    def cache_copies(row, blk, into):
        keys = pl.ds(pl.multiple_of((nkb - 1 - blk) * tp, tp), tp)
        return (pltpu.make_async_copy(ck_hbm.at[row, keys], kbuf_ref.at[into], sem.at[into, 0]),
                pltpu.make_async_copy(cv_hbm.at[row, keys], vbuf_ref.at[into], sem.at[into, 1]))

    def attend(n, nk, keys_of, values_of, u, masked):
        for h in range(n_heads):
            z_ref[head_rows[h], 0:n] = lax.dot_general(q_ref[h], keys_of(h), nt,
                                                      preferred_element_type=F32)
        for sub in range(n // nk - 1, -1, -1):
            cols = slice(sub * nk, (sub + 1) * nk)
            for rows in chunks:
                valid = None
                if masked:
                    row = lax.broadcasted_iota(jnp.int32, (rc, nk), 0) % s
                    valid = lax.broadcasted_iota(jnp.int32, (rc, nk), 1) < row
                log_keep, log_beta, first = _sb_logs(z_ref[rows, cols], valid)
                keep_ref[rows, 0:nk] = log_keep
                logb_ref[rows, 0:nk] = log_beta
                carry_ref[1, rows, :] = first
            later_ref[:, 0:nk] = jnp.dot(keep_ref[:, 0:nk], u, preferred_element_type=F32)
            for rows in chunks:
                later = later_ref[rows, 0:nk]
                carry = carry_ref[0, rows, :]
                w_ref[rows, cols] = jnp.exp(logb_ref[rows, 0:nk] + later + carry).astype(BF16)
                carry_ref[0, rows, :] = carry + later[:, 0:1] + carry_ref[1, rows, :]
        for h in range(n_heads):
            acc_ref[head_rows[h], :] += jnp.dot(w_ref[head_rows[h], 0:n], values_of(h),
                                                preferred_element_type=F32)

    def attend_cache_block():
        kc = 16
        for c in range(0, tp, kc):
            kt_ref[:, c:c + kc, :] = jnp.swapaxes(kbuf_ref[slot, c:c + kc], 0, 1).astype(BF16)
            vt_ref[:, c:c + kc, :] = jnp.swapaxes(vbuf_ref[slot, c:c + kc], 0, 1).astype(BF16)
        attend(tp, tk, lambda h: kt_ref[h], lambda h: vt_ref[h], _suffix_matrix(tk), False)

    @pl.when(b == 0)
    def _():
        for copy in cache_copies(0, 0, 0):
            copy.start()

    @pl.when(b + 1 < pl.num_programs(0))
    def _():
        for copy in cache_copies(b + 1, 0, 1 - slot):
            copy.start()

    carry_ref[...] = jnp.zeros_like(carry_ref)
    acc_ref[...] = jnp.zeros_like(acc_ref)
    zeros = jnp.zeros((pad - s, HEAD_DIM), BF16)
    attend(pad, pad,
           lambda h: jnp.concatenate([kn_ref[h], zeros], axis=0),
           lambda h: jnp.concatenate([vn_ref[h], zeros], axis=0),
           _suffix_matrix(pad), True)

    for copy in cache_copies(b, 0, slot):
        copy.wait()
    attend_cache_block()

    def stick_left():
        return jnp.max(carry_ref[0]) > STICK_GONE

    def older(state):
        blk, _ = state
        copies = cache_copies(b, blk, slot)
        for copy in copies:
            copy.start()
        for copy in copies:
            copy.wait()
        attend_cache_block()
        return blk + 1, stick_left()

    lax.while_loop(lambda state: (state[0] < nkb) & state[1], older, (jnp.int32(1), stick_left()))

    for h in range(n_heads):
        o_ref[:, h * HEAD_DIM:(h + 1) * HEAD_DIM] = acc_ref[head_rows[h], :].astype(o_ref.dtype)


def _attn_sample(qh, kh, vh, cache_k, cache_v, s, tp, tk, rc):
    h, t, hd = qh.shape
    bn, past = cache_k.shape[:2]
    m = h * s
    assert rc % s == 0 and m % rc == 0 and past % tp == 0 and tp % tk == 0 and hd <= tk
    new = pl.BlockSpec((h, s, hd), lambda b: (0, b, 0))
    hbm = pl.BlockSpec(memory_space=pl.ANY)
    return pl.pallas_call(
        functools.partial(_attn_sample_kernel, s=s, tp=tp, tk=tk, pad=hd, rc=rc),
        grid=(bn,),
        in_specs=[new, new, new, hbm, hbm],
        out_specs=pl.BlockSpec((s, h * hd), lambda b: (b, 0)),
        out_shape=jax.ShapeDtypeStruct((t, h * hd), BF16),
        scratch_shapes=[
            pltpu.VMEM((2, tp, h, hd), F32),
            pltpu.VMEM((2, tp, h, hd), F32),
            pltpu.SemaphoreType.DMA((2, 2)),
            pltpu.VMEM((h, tp, hd), BF16),
            pltpu.VMEM((h, tp, hd), BF16),
            pltpu.VMEM((m, tp), F32),
            pltpu.VMEM((m, tk), BF16),
            pltpu.VMEM((m, tk), F32),
            pltpu.VMEM((m, tk), F32),
            pltpu.VMEM((m, tp), BF16),
            pltpu.VMEM((m, hd), F32),
            pltpu.VMEM((2, m, 1), F32),
        ],
        compiler_params=_params("arbitrary"),
        name="attn_sample",
    )(qh, kh, vh, cache_k, cache_v)


def _trunk(x, conv_buf, h0, pos0, cache, p):
    bn, s, d = x.shape
    t = bn * s
    x0 = x.reshape(t, d)
    tl = _tiles(t, s)

    gate, rec = _in_proj(x0, p["a_norm"], p["a_w_in"], tl.rows_wide, tl.cols)
    hg, h_last, new_buf = _rglru(
        rec.reshape(bn, s, d), gate.reshape(bn, s, d), conv_buf, h0.reshape(bn, 1, d),
        p["a_conv_w"], p["a_conv_b"], p["a_b_r"], p["a_b_i"], p["a_lambda"], p["a_w_r"], p["a_w_i"],
        pos0, tl.scan_rows, tl.scan_cols)
    x1 = _matmul_res(hg.reshape(t, d), p["a_w_out"], x0, tl.rows, d)
    x2 = _mlp(x1, p["mlp_norm0"], p["mlp_w_up"], p["mlp_w_down"], 0, tl.rows_wide, tl.cols)

    k, v, kh, vh, qh = _kvq(x2, p["kv_norm"], p["b_norm"], p["w_kv"], p["b_w_q"],
                            p["k_norm"], p["b_q_norm"], tl.rows_wide, tl.cols // 2)
    if cache is None:
        o = _attn_rows(qh, kh, vh, ATTN_TILE, ATTN_TILES, CHUNK_ROWS, ATTN_HEADS)
    else:
        o = _attn_sample(qh, kh, vh, cache[0], cache[1], s, CACHE_BLOCK, CACHE_BLOCK, CHUNK_ROWS)
    x3 = _matmul_res(o, p["b_w_o"], x2, tl.rows, d)
    y = _mlp(x3, p["mlp_norm1"], p["mlp_w_up"], p["mlp_w_down"], 1, tl.rows_wide, tl.cols)

    return (y.reshape(bn, s, d), h_last.reshape(1, bn, d), new_buf.reshape(1, bn, CONV_W - 1, d),
            k.reshape(bn, s, N_HEADS, HEAD_DIM), v.reshape(bn, s, N_HEADS, HEAD_DIM))


def kernel(x_prompt, x_sample, state_lru_h, state_conv, cache_k, cache_v, a_norm, a_w_in, a_conv_w, a_conv_b, a_w_r, a_b_r, a_w_i, a_b_i, a_lambda, a_w_out, kv_norm, w_kv, k_norm, b_norm, b_w_q, b_q_norm, b_w_o, mlp_norm, mlp_w_up, mlp_w_down):
    row = lambda a: a.reshape(1, -1)
    p = {
        "a_norm": row(a_norm[0]), "a_w_in": a_w_in[0].astype(BF16),
        "a_conv_w": a_conv_w[0], "a_conv_b": row(a_conv_b[0]),
        "a_w_r": a_w_r[0].astype(BF16), "a_b_r": row(a_b_r[0]),
        "a_w_i": a_w_i[0].astype(BF16), "a_b_i": row(a_b_i[0]),
        "a_lambda": row(a_lambda[0]), "a_w_out": a_w_out[0].astype(BF16),
        "kv_norm": row(kv_norm), "w_kv": w_kv.astype(BF16), "k_norm": row(k_norm),
        "b_norm": row(b_norm[0]), "b_w_q": b_w_q[0].astype(BF16), "b_q_norm": row(b_q_norm[0]),
        "b_w_o": b_w_o[0].astype(BF16),
        "mlp_norm0": row(mlp_norm[0]), "mlp_norm1": row(mlp_norm[1]),
        "mlp_w_up": mlp_w_up.astype(BF16), "mlp_w_down": mlp_w_down.astype(BF16),
    }
    bp = x_prompt.shape[0]
    zero_conv = jnp.zeros((bp, CONV_W - 1, D_RNN), x_prompt.dtype)
    zero_h = jnp.zeros((bp, D_RNN), x_prompt.dtype)
    y_p, p_h, p_conv, p_k, p_v = _trunk(x_prompt, zero_conv, zero_h, 0, None, p)
    y_s, s_h, s_conv, s_k, s_v = _trunk(x_sample, state_conv[0], state_lru_h[0], PAST_LEN,
                                        (cache_k, cache_v), p)
    return (y_p, y_s, p_h, p_conv, p_k, p_v, s_h, s_conv, s_k, s_v)
```

```python
import functools
from typing import NamedTuple

import jax
import jax.numpy as jnp
from jax import lax
from jax.experimental import pallas as pl
from jax.experimental.pallas import tpu as pltpu

F32 = jnp.float32
BF16 = jnp.bfloat16

D_MODEL = 2048
D_RNN = D_MODEL
N_HEADS = 16
HEAD_DIM = D_MODEL // N_HEADS
N_LRU_HEADS = 16
LRU_BLOCK = D_RNN // N_LRU_HEADS
CONV_W = 4
LRU_C = 8.0
D_FF = 4 * D_MODEL
PAST_LEN = 2048
EPS = 1e-6

V7X_VMEM_LIMIT_BYTES = 56 * 1024 * 1024
SUBLANES = 8

ROW_TILE = 512
SCAN_TILE = 256
ATTN_TILE = 256
ATTN_HEADS = 4
ATTN_TILES = 1
CACHE_BLOCK = 256
CHUNK_ROWS = 32


class _Tiles(NamedTuple):
    rows: int
    rows_wide: int
    cols: int
    scan_rows: int
    scan_cols: int


def _tiles(t, s):
    one_row_tile = t <= ROW_TILE
    one_time_tile = s <= SCAN_TILE
    return _Tiles(rows=min(ROW_TILE, t), rows_wide=min(2 * ROW_TILE, t),
                  cols=1024 if one_row_tile else 512,
                  scan_rows=min(SCAN_TILE, s), scan_cols=D_RNN if one_time_tile else 1024)


def _params(*sem):
    return pltpu.CompilerParams(dimension_semantics=sem, vmem_limit_bytes=V7X_VMEM_LIMIT_BYTES)


def _rmsnorm(x, g):
    y = x * lax.rsqrt(jnp.mean(x * x, axis=-1, keepdims=True) + EPS)
    return y * g


def _row_chunks(n, size=128):
    return [slice(r0, min(r0 + size, n)) for r0 in range(0, n, size)]


def _softplus(x):
    return jnp.maximum(x, 0.0) + jnp.log(1.0 + jnp.exp(-jnp.abs(x)))


def _head_rmsnorm(t, g):
    outs = []
    for k in range(t.shape[1] // HEAD_DIM):
        outs.append(_rmsnorm(t[:, k * HEAD_DIM:(k + 1) * HEAD_DIM], g))
    return jnp.concatenate(outs, axis=1) if len(outs) > 1 else outs[0]


def _in_proj_kernel(x_ref, g_ref, wg_ref, wr_ref, gate_ref, rec_ref, xn_ref):
    @pl.when(pl.program_id(1) == 0)
    def _():
        for rows in _row_chunks(x_ref.shape[0]):
            xn_ref[rows, :] = _rmsnorm(x_ref[rows, :], g_ref[...]).astype(BF16)

    xn = xn_ref[...]
    gate_ref[...] = jax.nn.gelu(jnp.dot(xn, wg_ref[...], preferred_element_type=F32))
    rec_ref[...] = jnp.dot(xn, wr_ref[...], preferred_element_type=F32)


def _in_proj(x, g, w, tm, tn):
    t, d = x.shape
    nh = D_RNN // tn
    return pl.pallas_call(
        _in_proj_kernel,
        grid=(t // tm, nh),
        in_specs=[
            pl.BlockSpec((tm, d), lambda i, j: (i, 0)),
            pl.BlockSpec((1, d), lambda i, j: (0, 0)),
            pl.BlockSpec((d, tn), lambda i, j: (0, j)),
            pl.BlockSpec((d, tn), lambda i, j: (0, j + nh)),
        ],
        out_specs=[
            pl.BlockSpec((tm, tn), lambda i, j: (i, j)),
            pl.BlockSpec((tm, tn), lambda i, j: (i, j)),
        ],
        out_shape=[jax.ShapeDtypeStruct((t, D_RNN), F32)] * 2,
        scratch_shapes=[pltpu.VMEM((tm, d), BF16)],
        compiler_params=_params("parallel", "arbitrary"),
        name="in_proj",
    )(x, g, w, w)


def _rglru_kernel(rec_ref, gate_ref, cbuf_ref, h0_ref, cw_ref, cb_ref, br_ref, bi_ref, lam_ref,
                  wr_ref, wi_ref, hg_ref, hlast_ref, nbuf_ref, ext_ref, hc_ref, *, ts, tc, pos0):
    t = pl.program_id(2)
    pad = SUBLANES

    @pl.when(t == 0)
    def _():
        ext_ref[pad - (CONV_W - 1):pad, :] = cbuf_ref[...]
        hc_ref[...] = h0_ref[...]

    ext_ref[pad:pad + ts, :] = rec_ref[...]
    c = cb_ref[...] + ext_ref[pad - 3:pad - 3 + ts, :] * cw_ref[0:1, :]
    for j in range(1, CONV_W):
        c = c + ext_ref[pad - 3 + j:pad - 3 + j + ts, :] * cw_ref[j:j + 1, :]

    c16 = c.astype(BF16)
    rs, is_ = [], []
    for k in range(tc // LRU_BLOCK):
        blk = c16[:, k * LRU_BLOCK:(k + 1) * LRU_BLOCK]
        rs.append(jnp.dot(blk, wr_ref[k], preferred_element_type=F32))
        is_.append(jnp.dot(blk, wi_ref[k], preferred_element_type=F32))
    r = jax.nn.sigmoid(jnp.concatenate(rs, axis=1) + br_ref[...])
    ig = jax.nn.sigmoid(jnp.concatenate(is_, axis=1) + bi_ref[...])

    log_a = (-LRU_C * r) * _softplus(-lam_ref[...])
    a = jnp.exp(log_a)
    m2 = jnp.tanh(-log_a) * (a * a + 1.0)
    mult = m2 * lax.rsqrt(jnp.maximum(m2, jnp.finfo(F32).tiny))
    row = lax.broadcasted_iota(jnp.int32, (ts, tc), 0)
    if pos0 == 0:
        mult = jnp.where((row == 0) & (t == 0), 1.0, mult)
    b = mult * ig * c

    groups = ts // SUBLANES
    a = a.reshape(groups, SUBLANES, tc)
    b = b.reshape(groups, SUBLANES, tc)
    sub = lax.broadcasted_iota(jnp.int32, (groups, SUBLANES, tc), 1)
    d = 1
    while d < SUBLANES:
        keep = sub >= d
        a_sh = jnp.where(keep, pltpu.roll(a, d, 1), 1.0)
        b_sh = jnp.where(keep, pltpu.roll(b, d, 1), 0.0)
        b = a * b_sh + b
        a = a * a_sh
        d *= 2
    state = hc_ref[...]
    for g in range(0, groups, 2):
        h0 = b[g] + a[g] * state
        h1 = b[g + 1] + a[g + 1] * h0[SUBLANES - 1:SUBLANES, :]
        state = h1[SUBLANES - 1:SUBLANES, :]
        rows = slice(g * SUBLANES, (g + 2) * SUBLANES)
        h = jnp.concatenate([h0, h1], axis=0)
        hg_ref[rows, :] = (h * gate_ref[rows, :]).astype(hg_ref.dtype)

    hc_ref[...] = state
    hlast_ref[...] = state
    nbuf_ref[...] = ext_ref[pad + ts - (CONV_W - 1):pad + ts, :]
    ext_ref[0:pad, :] = ext_ref[ts:ts + pad, :]


def _rglru(rec, gate, cbuf, h0, cw, cb, br, bi, lam, wr, wi, pos0, ts, tc):
    bn, s, c = rec.shape
    nblk = tc // LRU_BLOCK
    seq = pl.BlockSpec((None, ts, tc), lambda b, ci, ti: (b, ti, ci))
    vec = pl.BlockSpec((1, tc), lambda b, ci, ti: (0, ci))
    return pl.pallas_call(
        functools.partial(_rglru_kernel, ts=ts, tc=tc, pos0=pos0),
        grid=(bn, c // tc, s // ts),
        in_specs=[
            seq, seq,
            pl.BlockSpec((None, CONV_W - 1, tc), lambda b, ci, ti: (b, 0, ci)),
            pl.BlockSpec((None, 1, tc), lambda b, ci, ti: (b, 0, ci)),
            pl.BlockSpec((CONV_W, tc), lambda b, ci, ti: (0, ci)),
            vec, vec, vec, vec,
            pl.BlockSpec((nblk, LRU_BLOCK, LRU_BLOCK), lambda b, ci, ti: (ci, 0, 0)),
            pl.BlockSpec((nblk, LRU_BLOCK, LRU_BLOCK), lambda b, ci, ti: (ci, 0, 0)),
        ],
        out_specs=[
            seq,
            pl.BlockSpec((None, 1, tc), lambda b, ci, ti: (b, 0, ci)),
            pl.BlockSpec((None, CONV_W - 1, tc), lambda b, ci, ti: (b, 0, ci)),
        ],
        out_shape=[
            jax.ShapeDtypeStruct((bn, s, c), BF16),
            jax.ShapeDtypeStruct((bn, 1, c), F32),
            jax.ShapeDtypeStruct((bn, CONV_W - 1, c), F32),
        ],
        scratch_shapes=[pltpu.VMEM((ts + 2 * SUBLANES, tc), F32), pltpu.VMEM((1, tc), F32)],
        compiler_params=_params("parallel", "parallel", "arbitrary"),
        name="rglru",
    )(rec, gate, cbuf, h0, cw, cb, br, bi, lam, wr, wi)


def _matmul_res_kernel(a_ref, w_ref, res_ref, o_ref):
    o_ref[...] = res_ref[...] + jnp.dot(a_ref[...], w_ref[...], preferred_element_type=F32)


def _matmul_res(a, w, res, tm, tn):
    t, k = a.shape
    n = w.shape[1]
    return pl.pallas_call(
        _matmul_res_kernel,
        grid=(t // tm, n // tn),
        in_specs=[
            pl.BlockSpec((tm, k), lambda i, j: (i, 0)),
            pl.BlockSpec((k, tn), lambda i, j: (0, j)),
            pl.BlockSpec((tm, tn), lambda i, j: (i, j)),
        ],
        out_specs=pl.BlockSpec((tm, tn), lambda i, j: (i, j)),
        out_shape=jax.ShapeDtypeStruct((t, n), F32),
        compiler_params=_params("parallel", "arbitrary"),
        name="matmul_res",
    )(a, w, res)


def _mlp_kernel(x_ref, g_ref, wu_ref, wd_ref, o_ref, xn_ref):
    @pl.when(pl.program_id(1) == 0)
    def _():
        for rows in _row_chunks(x_ref.shape[0]):
            x = x_ref[rows, :]
            xn_ref[rows, :] = _rmsnorm(x, g_ref[...]).astype(BF16)
            o_ref[rows, :] = x

    h = jnp.maximum(jnp.dot(xn_ref[...], wu_ref[...], preferred_element_type=F32), 0.0)
    o_ref[...] += jnp.dot((h * h).astype(BF16), wd_ref[...], preferred_element_type=F32)


def _mlp(x, g, wu, wd, layer, tm, tf):
    t, d = x.shape
    f = wu.shape[2]
    return pl.pallas_call(
        _mlp_kernel,
        grid=(t // tm, f // tf),
        in_specs=[
            pl.BlockSpec((tm, d), lambda i, j: (i, 0)),
            pl.BlockSpec((1, d), lambda i, j: (0, 0)),
            pl.BlockSpec((None, d, tf), lambda i, j: (layer, 0, j)),
            pl.BlockSpec((None, tf, d), lambda i, j: (layer, j, 0)),
        ],
        out_specs=pl.BlockSpec((tm, d), lambda i, j: (i, 0)),
        out_shape=jax.ShapeDtypeStruct((t, d), F32),
        scratch_shapes=[pltpu.VMEM((tm, d), BF16)],
        compiler_params=_params("parallel", "arbitrary"),
        name="mlp",
    )(x, g, wu, wd)


def _kvq_kernel(x_ref, gkv_ref, gq_ref, wk_ref, wv_ref, wq_ref, kn_ref, qn_ref,
                k_ref, v_ref, kh_ref, vh_ref, qh_ref, xkv_ref, xq_ref):
    @pl.when(pl.program_id(1) == 0)
    def _():
        for rows in _row_chunks(x_ref.shape[0]):
            x = x_ref[rows, :]
            xhat = x * lax.rsqrt(jnp.mean(x * x, axis=-1, keepdims=True) + EPS)
            xkv_ref[rows, :] = (xhat * gkv_ref[...]).astype(BF16)
            xq_ref[rows, :] = (xhat * gq_ref[...]).astype(BF16)

    xkv = xkv_ref[...]
    k = _head_rmsnorm(jnp.dot(xkv, wk_ref[...], preferred_element_type=F32), kn_ref[...])
    v = jnp.dot(xkv, wv_ref[...], preferred_element_type=F32)
    q = _head_rmsnorm(jnp.dot(xq_ref[...], wq_ref[...], preferred_element_type=F32), qn_ref[...])
    k_ref[...] = k
    v_ref[...] = v
    for hh in range(kh_ref.shape[0]):
        sl = slice(hh * HEAD_DIM, (hh + 1) * HEAD_DIM)
        if kh_ref.shape[1] == HEAD_DIM:
            kh_ref[hh] = k[:, sl].T.astype(BF16)
        else:
            kh_ref[hh] = k[:, sl].astype(BF16)
        vh_ref[hh] = v[:, sl].astype(BF16)
        qh_ref[hh] = q[:, sl].astype(BF16)


def _kvq(x, gkv, gq, wkv, wq, kn, qn, tm, tn, keys_transposed):
    t, d = x.shape
    nh = D_MODEL // tn
    hb = tn // HEAD_DIM
    tile = pl.BlockSpec((tm, tn), lambda i, j: (i, j))
    heads = pl.BlockSpec((hb, tm, HEAD_DIM), lambda i, j: (j, i, 0))
    if keys_transposed:
        key_heads = pl.BlockSpec((hb, HEAD_DIM, tm), lambda i, j: (j, 0, i))
        key_shape = jax.ShapeDtypeStruct((N_HEADS, HEAD_DIM, t), BF16)
    else:
        key_heads = heads
        key_shape = jax.ShapeDtypeStruct((N_HEADS, t, HEAD_DIM), BF16)
    vec = pl.BlockSpec((1, d), lambda i, j: (0, 0))
    hvec = pl.BlockSpec((1, HEAD_DIM), lambda i, j: (0, 0))
    return pl.pallas_call(
        _kvq_kernel,
        grid=(t // tm, nh),
        in_specs=[
            pl.BlockSpec((tm, d), lambda i, j: (i, 0)),
            vec, vec,
            pl.BlockSpec((d, tn), lambda i, j: (0, j)),
            pl.BlockSpec((d, tn), lambda i, j: (0, j + nh)),
            pl.BlockSpec((d, tn), lambda i, j: (0, j)),
            hvec, hvec,
        ],
        out_specs=[tile, tile, key_heads, heads, heads],
        out_shape=[
            jax.ShapeDtypeStruct((t, D_MODEL), F32),
            jax.ShapeDtypeStruct((t, D_MODEL), F32),
            key_shape,
            jax.ShapeDtypeStruct((N_HEADS, t, HEAD_DIM), BF16),
            jax.ShapeDtypeStruct((N_HEADS, t, HEAD_DIM), BF16),
        ],
        scratch_shapes=[pltpu.VMEM((tm, d), BF16), pltpu.VMEM((tm, d), BF16)],
        compiler_params=_params("parallel", "arbitrary"),
        name="kvq",
    )(x, gkv, gq, wkv, wkv, wq, kn, qn)


def _suffix_matrix(n):
    r = lax.broadcasted_iota(jnp.int32, (n, n), 0)
    c = lax.broadcasted_iota(jnp.int32, (n, n), 1)
    return jnp.where(r > c, 1.0, 0.0).astype(BF16)


NEG_BIG = -1e30
STICK_GONE = -110.0
LOG2E = 1.4426950408889634


def _sb_logs(zr, valid):
    scale = HEAD_DIM ** -0.5
    nz = zr * (-scale)
    tail = jnp.log(1.0 + jnp.exp2(jnp.abs(zr) * (-scale * LOG2E)))
    log_keep = jnp.minimum(nz, 0.0) - tail
    log_beta = log_keep - nz
    if valid is not None:
        log_keep = jnp.where(valid, log_keep, 0.0)
        log_beta = jnp.where(valid, log_beta, NEG_BIG)
    return log_keep.astype(BF16), log_beta, log_keep[:, 0:1]


def _attn_rows_kernel(q_ref, k_ref, v_ref, o_ref, z_ref, keep_ref, logb_ref, first_ref, later_ref,
                      w_ref, acc_ref, carry_ref, *, tb, nrt, rc, hb):
    qi = pl.program_id(1)
    u = _suffix_matrix(tb)
    chunks = [slice(r0, r0 + rc) for r0 in range(0, tb, rc)]
    chains = [(g, rt) for g in range(hb) for rt in range(nrt)]

    def tile_block(rt):
        return qi * nrt + rt

    def attend(g, rt, blk0, nblk, diagonal):
        c = g * nrt + rt
        n = nblk * tb
        ks = pl.ds(pl.multiple_of(blk0 * tb, tb), n)
        q = q_ref[g, rt * tb:(rt + 1) * tb, :]
        z_ref[c, :, 0:n] = jnp.dot(q, k_ref[g, :, ks], preferred_element_type=F32)
        for blk in range(nblk - 1, -1, -1):
            cols = slice(blk * tb, (blk + 1) * tb)
            for rows in chunks:
                valid = None
                if diagonal and blk == nblk - 1:
                    row = lax.broadcasted_iota(jnp.int32, (rc, tb), 0) + rows.start
                    valid = lax.broadcasted_iota(jnp.int32, (rc, tb), 1) < row
                log_keep, log_beta, first = _sb_logs(z_ref[c, rows, cols], valid)
                keep_ref[c, rows, cols] = log_keep
                logb_ref[c, rows, cols] = log_beta
                first_ref[c, blk, rows, :] = first
            later_ref[c, :, cols] = jnp.dot(keep_ref[c, :, cols], u, preferred_element_type=F32)
            for rows in chunks:
                later = later_ref[c, rows, cols]
                carry = carry_ref[c, rows, :]
                w_ref[c, rows, cols] = jnp.exp(logb_ref[c, rows, cols] + later + carry).astype(BF16)
                carry_ref[c, rows, :] = carry + later[:, 0:1] + first_ref[c, blk, rows, :]
        acc_ref[c] += jnp.dot(w_ref[c, :, 0:n], v_ref[g, ks, :], preferred_element_type=F32)

    carry_ref[...] = jnp.zeros_like(carry_ref)
    acc_ref[...] = jnp.zeros_like(acc_ref)

    @pl.when(qi == 0)
    def _():
        for g, rt in chains:
            attend(g, rt, max(rt - 1, 0), min(rt + 1, 2), True)

    @pl.when(qi > 0)
    def _():
        for g, rt in chains:
            attend(g, rt, tile_block(rt) - 1, 2, True)

    def stick_left():
        return jnp.max(carry_ref[...]) > STICK_GONE

    def earlier(state):
        m, _ = state
        for rt in range(nrt):
            blk = tile_block(rt) - 2 - m

            @pl.when(blk >= 0)
            def _():
                for g in range(hb):
                    attend(g, rt, blk, 1, False)
        return m + 1, stick_left()

    trips = tile_block(nrt - 1) - 1
    lax.while_loop(lambda state: (state[0] < trips) & state[1], earlier,
                   (jnp.int32(0), stick_left()))

    for g, rt in chains:
        o_ref[rt * tb:(rt + 1) * tb, g * HEAD_DIM:(g + 1) * HEAD_DIM] = (
            acc_ref[g * nrt + rt].astype(o_ref.dtype))


def _attn_rows(qh, kh, vh, tb, nrt, rc, hb):
    h, t, hd = qh.shape
    tq = nrt * tb
    nc = hb * nrt
    assert h % hb == 0 and t % tq == 0 and tb % rc == 0
    resident = pl.BlockSpec((hb, t, hd), lambda hi, qi: (hi, 0, 0), pipeline_mode=pl.Buffered(1))
    resident_t = pl.BlockSpec((hb, hd, t), lambda hi, qi: (hi, 0, 0), pipeline_mode=pl.Buffered(1))
    return pl.pallas_call(
        functools.partial(_attn_rows_kernel, tb=tb, nrt=nrt, rc=rc, hb=hb),
        grid=(h // hb, t // tq),
        in_specs=[pl.BlockSpec((hb, tq, hd), lambda hi, qi: (hi, qi, 0)), resident_t, resident],
        out_specs=pl.BlockSpec((tq, hb * hd), lambda hi, qi: (qi, hi)),
        out_shape=jax.ShapeDtypeStruct((t, h * hd), BF16),
        scratch_shapes=[
            pltpu.VMEM((nc, tb, 2 * tb), F32),
            pltpu.VMEM((nc, tb, 2 * tb), BF16),
            pltpu.VMEM((nc, tb, 2 * tb), F32),
            pltpu.VMEM((nc, 2, tb, 1), F32),
            pltpu.VMEM((nc, tb, 2 * tb), F32),
            pltpu.VMEM((nc, tb, 2 * tb), BF16),
            pltpu.VMEM((nc, tb, hd), F32),
            pltpu.VMEM((nc, tb, 1), F32),
        ],
        compiler_params=_params("parallel", "arbitrary"),
        name="attn_prompt",
    )(qh, kh, vh)


def _attn_sample_kernel(q_ref, kn_ref, vn_ref, ck_hbm, cv_hbm, o_ref,
                        kbuf_ref, vbuf_ref, sem, kt_ref, vt_ref, z_ref, keep_ref, logb_ref,
                        later_ref, w_ref, acc_ref, carry_ref, *, s, tp, tk, pad, rc):
    b = pl.program_id(0)
    n_heads = q_ref.shape[0]
    nkb = ck_hbm.shape[1] // tp
    m = n_heads * s
    chunks = [slice(r0, r0 + rc) for r0 in range(0, m, rc)]
    head_rows = [slice(h * s, (h + 1) * s) for h in range(n_heads)]
    nt = (((1,), (1,)), ((), ()))
    slot = b % 2

    def cache_copies(row, blk, into):
        keys = pl.ds(pl.multiple_of((nkb - 1 - blk) * tp, tp), tp)
        return (pltpu.make_async_copy(ck_hbm.at[row, keys], kbuf_ref.at[into], sem.at[into, 0]),
                pltpu.make_async_copy(cv_hbm.at[row, keys], vbuf_ref.at[into], sem.at[into, 1]))

    def attend(n, nk, keys_of, values_of, u, masked):
        for h in range(n_heads):
            z_ref[head_rows[h], 0:n] = lax.dot_general(q_ref[h], keys_of(h), nt,
                                                      preferred_element_type=F32)
        for sub in range(n // nk - 1, -1, -1):
            cols = slice(sub * nk, (sub + 1) * nk)
            for rows in chunks:
                valid = None
                if masked:
                    row = lax.broadcasted_iota(jnp.int32, (rc, nk), 0) % s
                    valid = lax.broadcasted_iota(jnp.int32, (rc, nk), 1) < row
                log_keep, log_beta, first = _sb_logs(z_ref[rows, cols], valid)
                keep_ref[rows, 0:nk] = log_keep
                logb_ref[rows, 0:nk] = log_beta
                carry_ref[1, rows, :] = first
            later_ref[:, 0:nk] = jnp.dot(keep_ref[:, 0:nk], u, preferred_element_type=F32)
            for rows in chunks:
                later = later_ref[rows, 0:nk]
                carry = carry_ref[0, rows, :]
                w_ref[rows, cols] = jnp.exp(logb_ref[rows, 0:nk] + later + carry).astype(BF16)
                carry_ref[0, rows, :] = carry + later[:, 0:1] + carry_ref[1, rows, :]
        for h in range(n_heads):
            acc_ref[head_rows[h], :] += jnp.dot(w_ref[head_rows[h], 0:n], values_of(h),
                                                preferred_element_type=F32)

    def attend_cache_block():
        kc = 16
        for c in range(0, tp, kc):
            kt_ref[:, c:c + kc, :] = jnp.swapaxes(kbuf_ref[slot, c:c + kc], 0, 1).astype(BF16)
            vt_ref[:, c:c + kc, :] = jnp.swapaxes(vbuf_ref[slot, c:c + kc], 0, 1).astype(BF16)
        attend(tp, tk, lambda h: kt_ref[h], lambda h: vt_ref[h], _suffix_matrix(tk), False)

    @pl.when(b == 0)
    def _():
        for copy in cache_copies(0, 0, 0):
            copy.start()

    @pl.when(b + 1 < pl.num_programs(0))
    def _():
        for copy in cache_copies(b + 1, 0, 1 - slot):
            copy.start()

    carry_ref[...] = jnp.zeros_like(carry_ref)
    acc_ref[...] = jnp.zeros_like(acc_ref)
    zeros = jnp.zeros((pad - s, HEAD_DIM), BF16)
    attend(pad, pad,
           lambda h: jnp.concatenate([kn_ref[h], zeros], axis=0),
           lambda h: jnp.concatenate([vn_ref[h], zeros], axis=0),
           _suffix_matrix(pad), True)

    for copy in cache_copies(b, 0, slot):
        copy.wait()
    attend_cache_block()

    def stick_left():
        return jnp.max(carry_ref[0]) > STICK_GONE

    def older(state):
        blk, _ = state
        copies = cache_copies(b, blk, slot)
        for copy in copies:
            copy.start()
        for copy in copies:
            copy.wait()
        attend_cache_block()
        return blk + 1, stick_left()

    lax.while_loop(lambda state: (state[0] < nkb) & state[1], older, (jnp.int32(1), stick_left()))

    for h in range(n_heads):
        o_ref[:, h * HEAD_DIM:(h + 1) * HEAD_DIM] = acc_ref[head_rows[h], :].astype(o_ref.dtype)


def _attn_sample(qh, kh, vh, cache_k, cache_v, s, tp, tk, rc):
    h, t, hd = qh.shape
    bn, past = cache_k.shape[:2]
    m = h * s
    assert rc % s == 0 and m % rc == 0 and past % tp == 0 and tp % tk == 0 and hd <= tk
    new = pl.BlockSpec((h, s, hd), lambda b: (0, b, 0))
    hbm = pl.BlockSpec(memory_space=pl.ANY)
    return pl.pallas_call(
        functools.partial(_attn_sample_kernel, s=s, tp=tp, tk=tk, pad=hd, rc=rc),
        grid=(bn,),
        in_specs=[new, new, new, hbm, hbm],
        out_specs=pl.BlockSpec((s, h * hd), lambda b: (b, 0)),
        out_shape=jax.ShapeDtypeStruct((t, h * hd), BF16),
        scratch_shapes=[
            pltpu.VMEM((2, tp, h, hd), F32),
            pltpu.VMEM((2, tp, h, hd), F32),
            pltpu.SemaphoreType.DMA((2, 2)),
            pltpu.VMEM((h, tp, hd), BF16),
            pltpu.VMEM((h, tp, hd), BF16),
            pltpu.VMEM((m, tp), F32),
            pltpu.VMEM((m, tk), BF16),
            pltpu.VMEM((m, tk), F32),
            pltpu.VMEM((m, tk), F32),
            pltpu.VMEM((m, tp), BF16),
            pltpu.VMEM((m, hd), F32),
            pltpu.VMEM((2, m, 1), F32),
        ],
        compiler_params=_params("arbitrary"),
        name="attn_sample",
    )(qh, kh, vh, cache_k, cache_v)


def _trunk(x, conv_buf, h0, pos0, cache, p):
    bn, s, d = x.shape
    t = bn * s
    x0 = x.reshape(t, d)
    tl = _tiles(t, s)

    gate, rec = _in_proj(x0, p["a_norm"], p["a_w_in"], tl.rows_wide, tl.cols)
    hg, h_last, new_buf = _rglru(
        rec.reshape(bn, s, d), gate.reshape(bn, s, d), conv_buf, h0.reshape(bn, 1, d),
        p["a_conv_w"], p["a_conv_b"], p["a_b_r"], p["a_b_i"], p["a_lambda"], p["a_w_r"], p["a_w_i"],
        pos0, tl.scan_rows, tl.scan_cols)
    x1 = _matmul_res(hg.reshape(t, d), p["a_w_out"], x0, tl.rows, d)
    x2 = _mlp(x1, p["mlp_norm0"], p["mlp_w_up"], p["mlp_w_down"], 0, tl.rows_wide, tl.cols)

    k, v, kh, vh, qh = _kvq(x2, p["kv_norm"], p["b_norm"], p["w_kv"], p["b_w_q"],
                            p["k_norm"], p["b_q_norm"], tl.rows_wide, tl.cols // 2, cache is None)
    if cache is None:
        o = _attn_rows(qh, kh, vh, ATTN_TILE, ATTN_TILES, CHUNK_ROWS, ATTN_HEADS)
    else:
        o = _attn_sample(qh, kh, vh, cache[0], cache[1], s, CACHE_BLOCK, CACHE_BLOCK, CHUNK_ROWS)
    x3 = _matmul_res(o, p["b_w_o"], x2, tl.rows, d)
    y = _mlp(x3, p["mlp_norm1"], p["mlp_w_up"], p["mlp_w_down"], 1, tl.rows_wide, tl.cols)

    return (y.reshape(bn, s, d), h_last.reshape(1, bn, d), new_buf.reshape(1, bn, CONV_W - 1, d),
            k.reshape(bn, s, N_HEADS, HEAD_DIM), v.reshape(bn, s, N_HEADS, HEAD_DIM))


def kernel(x_prompt, x_sample, state_lru_h, state_conv, cache_k, cache_v, a_norm, a_w_in, a_conv_w, a_conv_b, a_w_r, a_b_r, a_w_i, a_b_i, a_lambda, a_w_out, kv_norm, w_kv, k_norm, b_norm, b_w_q, b_q_norm, b_w_o, mlp_norm, mlp_w_up, mlp_w_down):
    row = lambda a: a.reshape(1, -1)
    p = {
        "a_norm": row(a_norm[0]), "a_w_in": a_w_in[0].astype(BF16),
        "a_conv_w": a_conv_w[0], "a_conv_b": row(a_conv_b[0]),
        "a_w_r": a_w_r[0].astype(BF16), "a_b_r": row(a_b_r[0]),
        "a_w_i": a_w_i[0].astype(BF16), "a_b_i": row(a_b_i[0]),
        "a_lambda": row(a_lambda[0]), "a_w_out": a_w_out[0].astype(BF16),
        "kv_norm": row(kv_norm), "w_kv": w_kv.astype(BF16), "k_norm": row(k_norm),
        "b_norm": row(b_norm[0]), "b_w_q": b_w_q[0].astype(BF16), "b_q_norm": row(b_q_norm[0]),
        "b_w_o": b_w_o[0].astype(BF16),
        "mlp_norm0": row(mlp_norm[0]), "mlp_norm1": row(mlp_norm[1]),
        "mlp_w_up": mlp_w_up.astype(BF16), "mlp_w_down": mlp_w_down.astype(BF16),
    }
    bp = x_prompt.shape[0]
    zero_conv = jnp.zeros((bp, CONV_W - 1, D_RNN), x_prompt.dtype)
    zero_h = jnp.zeros((bp, D_RNN), x_prompt.dtype)
    y_p, p_h, p_conv, p_k, p_v = _trunk(x_prompt, zero_conv, zero_h, 0, None, p)
    y_s, s_h, s_conv, s_k, s_v = _trunk(x_sample, state_conv[0], state_lru_h[0], PAST_LEN,
                                        (cache_k, cache_v), p)
    return (y_p, y_s, p_h, p_conv, p_k, p_v, s_h, s_conv, s_k, s_v)
```
